```python
import jax, jax.numpy as jnp
from jax import lax
import numpy as np

D_MODEL = 1024
BATCH = 8
SEQ = 4096
DEPTH = 4

GRID_W = 64
CTX_LEN = 256
MIX_WIDTH = D_MODEL
POOL_WIDTH = MIX_WIDTH // 2
N_POOL_GROUPS = 4
POOL_GROUP_DIM = POOL_WIDTH // N_POOL_GROUPS
POOL_WINDOWS = (2, 4, 8, 16)
ATTN_WIDTH = MIX_WIDTH - POOL_WIDTH
N_HEADS = 8
HEAD_DIM = ATTN_WIDTH // N_HEADS
NA_KH = 8
NA_KW = 16
NA_QB = 16
NA_KB = 32
NA_NCB = GRID_W // NA_QB
D_FF = 4 * D_MODEL
N_MOD = 6
EPS = 1e-6

kernel_name = 'hybrid_pool_natten_dit_block'


def _rmsnorm(x, g):
    x32 = x.astype(jnp.float32)
    y = x32 * lax.rsqrt(jnp.mean(x32 * x32, axis=-1, keepdims=True) + EPS)
    return (y * g.astype(jnp.float32)).astype(x.dtype)


def _modulate(x, gain, shift, scale):
    return _rmsnorm(x, gain) * (1 + scale) + shift


def _project(h, w_in_l):
    p = h @ w_in_l
    B, L, _ = p.shape
    u = p[..., :POOL_WIDTH]
    q, k, v = jnp.split(p[..., POOL_WIDTH:], 3, axis=-1)
    shp = (B, L, N_HEADS, HEAD_DIM)
    return u, q.reshape(shp), k.reshape(shp), v.reshape(shp)


def _pool_mixer(u, pool_w, pool_scale):
    B, L, _ = u.shape
    u32 = u.reshape(B, L, N_POOL_GROUPS, POOL_GROUP_DIM).astype(jnp.float32)
    cs = jnp.concatenate([jnp.zeros_like(u32[:, :1]), jnp.cumsum(u32, axis=1)], axis=1)
    t = jnp.arange(L)
    diffs = []
    for g, w in enumerate(POOL_WINDOWS):
        lo = w // 2
        hi = w - 1 - lo
        start = jnp.clip(t - lo, 0, L)
        end = jnp.clip(t + hi + 1, 0, L)
        count = (end - start).astype(jnp.float32)[None, :, None]
        diffs.append((cs[:, end, g] - cs[:, start, g]) / count - u32[:, :, g])
    d = jnp.stack(diffs, axis=2).astype(u.dtype)
    y = jnp.einsum('blgc,gcd->blgd', d, pool_w)
    return y.reshape(B, L, POOL_WIDTH) * pool_scale


def _na_column_tables():
    qcol = np.arange(GRID_W).reshape(NA_NCB, NA_QB)
    q_start = np.clip(qcol - NA_KW // 2, 0, GRID_W - NA_KW)
    kc0 = np.clip(np.arange(NA_NCB) * NA_QB - NA_KW // 2, 0, GRID_W - NA_KB)
    col_idx = kc0[:, None] + np.arange(NA_KB)
    kc = col_idx[:, None, :]
    mask = (kc >= q_start[:, :, None]) & (kc < q_start[:, :, None] + NA_KW)
    dc = np.clip(kc - qcol[:, :, None] + NA_KW - 1, 0, 2 * NA_KW - 2)
    return col_idx.astype(np.int32), mask, dc.astype(np.int32)


def _neighbourhood_attention(q, k, v, k_ctx, v_ctx, rpb):
    B, L, H, Dh = q.shape
    rows = L // GRID_W
    kh = min(NA_KH, rows)
    col_np, mask_np, dc_np = _na_column_tables()
    col_idx = jnp.asarray(col_np)
    col_mask = jnp.asarray(mask_np)[:, :, None, :]
    dc_idx = jnp.asarray(dc_np)[:, :, None, :]
    qg = q.reshape(B, rows, GRID_W, H, Dh)
    kg = k.reshape(B, rows, GRID_W, H, Dh)
    vg = v.reshape(B, rows, GRID_W, H, Dh)
    scale = Dh ** -0.5
    n_loc = kh * NA_KB

    def row_block(r):
        rs = jnp.clip(r - kh // 2, 0, rows - kh)
        q_r = lax.dynamic_index_in_dim(qg, r, axis=1, keepdims=False).reshape(B, NA_NCB, NA_QB, H, Dh)
        k_b = lax.dynamic_slice_in_dim(kg, rs, kh, axis=1)[:, :, col_idx]
        v_b = lax.dynamic_slice_in_dim(vg, rs, kh, axis=1)[:, :, col_idx]
        s_loc = jnp.einsum('bnqhd,brnkhd->bhnqrk', q_r, k_b).astype(jnp.float32) * scale
        dr_idx = rs + jnp.arange(kh) - r + (NA_KH - 1)
        bias = rpb[:, dr_idx[None, None, :, None], dc_idx]
        s_loc = jnp.where(col_mask, s_loc + bias.astype(jnp.float32), -jnp.inf)
        s_ctx = jnp.einsum('bnqhd,bkhd->bhnqk', q_r, k_ctx).astype(jnp.float32) * scale
        s = jnp.concatenate([s_loc.reshape(B, H, NA_NCB, NA_QB, n_loc), s_ctx], axis=-1)
        p = jax.nn.softmax(s, axis=-1).astype(v.dtype)
        p_loc = p[..., :n_loc].reshape(B, H, NA_NCB, NA_QB, kh, NA_KB)
        p_ctx = p[..., n_loc:]
        o = (jnp.einsum('bhnqrk,brnkhd->bnqhd', p_loc, v_b)
             + jnp.einsum('bhnqk,bkhd->bnqhd', p_ctx, v_ctx))
        return o.reshape(B, GRID_W, H, Dh)

    out = lax.map(row_block, jnp.arange(rows))
    return jnp.moveaxis(out, 0, 1).reshape(B, L, H * Dh)


def _context_attention(q, k, v):
    B, C, H, Dh = q.shape
    s = jnp.einsum('bqhd,bkhd->bhqk', q, k).astype(jnp.float32) * (Dh ** -0.5)
    p = jax.nn.softmax(s, axis=-1).astype(v.dtype)
    return jnp.einsum('bhqk,bkhd->bqhd', p, v).reshape(B, C, H * Dh)


def _sq_relu_mlp(h, w1, w2):
    return jnp.square(jax.nn.relu(h @ w1)) @ w2


def setup_inputs(seed: int = 0) -> dict:
    key = jax.random.key(seed)
    ks = jax.random.split(key, 17)
    D = D_MODEL
    nrm = jax.random.normal
    return {
        'x': nrm(ks[0], (BATCH, SEQ, D), jnp.float32),
        'c': nrm(ks[1], (BATCH, D), jnp.float32),
        'ctx': nrm(ks[2], (BATCH, CTX_LEN, D), jnp.float32),
        'c_ctx': nrm(ks[3], (D,), jnp.float32),
        'w_mod': nrm(ks[4], (DEPTH, D, N_MOD * D), jnp.float32) * (0.5 * D ** -0.5),
        'b_mod': nrm(ks[5], (DEPTH, N_MOD * D), jnp.float32) * 0.02,
        'g_pre_mix': 1.0 + 0.1 * nrm(ks[6], (DEPTH, D), jnp.float32),
        'g_post_mix': 1.0 + 0.1 * nrm(ks[7], (DEPTH, D), jnp.float32),
        'g_pre_mlp': 1.0 + 0.1 * nrm(ks[8], (DEPTH, D), jnp.float32),
        'g_post_mlp': 1.0 + 0.1 * nrm(ks[9], (DEPTH, D), jnp.float32),
        'w_in': nrm(ks[10], (DEPTH, D, POOL_WIDTH + 3 * ATTN_WIDTH), jnp.float32) * D ** -0.5,
        'pool_w': nrm(ks[11], (DEPTH, N_POOL_GROUPS, POOL_GROUP_DIM, POOL_GROUP_DIM), jnp.float32) * POOL_GROUP_DIM ** -0.5,
        'pool_scale': 1.0 + 0.1 * nrm(ks[12], (DEPTH, POOL_WIDTH), jnp.float32),
        'rpb': 0.1 * nrm(ks[13], (DEPTH, N_HEADS, 2 * NA_KH - 1, 2 * NA_KW - 1), jnp.float32),
        'w_out': nrm(ks[14], (DEPTH, MIX_WIDTH, D), jnp.float32) * MIX_WIDTH ** -0.5,
        'w_mlp_in': nrm(ks[15], (DEPTH, D, D_FF), jnp.float32) * D ** -0.5,
        'w_mlp_out': nrm(ks[16], (DEPTH, D_FF, D), jnp.float32) * D_FF ** -0.5,
    }


def reference(x, c, ctx, c_ctx, w_mod, b_mod, g_pre_mix, g_post_mix, g_pre_mlp, g_post_mlp,
              w_in, pool_w, pool_scale, rpb, w_out, w_mlp_in, w_mlp_out):
    for l in range(DEPTH):
        last = l == DEPTH - 1
        mod_x = (jax.nn.silu(c) @ w_mod[l] + b_mod[l])[:, None, :]
        mod_c = jax.nn.silu(c_ctx) @ w_mod[l] + b_mod[l]
        sh1, sc1, gt1, sh2, sc2, gt2 = jnp.split(mod_x, N_MOD, axis=-1)
        csh1, csc1, cgt1, csh2, csc2, cgt2 = jnp.split(mod_c, N_MOD, axis=-1)

        h_x = _modulate(x, g_pre_mix[l], sh1, sc1)
        h_c = _modulate(ctx, g_pre_mix[l], csh1, csc1)
        u_x, q_x, k_x, v_x = _project(h_x, w_in[l])
        u_c, q_c, k_c, v_c = _project(h_c, w_in[l])
        attn_x = _neighbourhood_attention(q_x, k_x, v_x, k_c, v_c, rpb[l])
        mix_x = jnp.concatenate([_pool_mixer(u_x, pool_w[l], pool_scale[l]), attn_x], axis=-1) @ w_out[l]
        x = x + gt1 * _rmsnorm(mix_x, g_post_mix[l])
        y_x = _sq_relu_mlp(_modulate(x, g_pre_mlp[l], sh2, sc2), w_mlp_in[l], w_mlp_out[l])
        x = x + gt2 * _rmsnorm(y_x, g_post_mlp[l])

        if not last:
            attn_c = _context_attention(q_c, k_c, v_c)
            mix_c = jnp.concatenate([_pool_mixer(u_c, pool_w[l], pool_scale[l]), attn_c], axis=-1) @ w_out[l]
            ctx = ctx + cgt1 * _rmsnorm(mix_c, g_post_mix[l])
            y_c = _sq_relu_mlp(_modulate(ctx, g_pre_mlp[l], csh2, csc2), w_mlp_in[l], w_mlp_out[l])
            ctx = ctx + cgt2 * _rmsnorm(y_c, g_post_mlp[l])
    return x
```

```python
import functools

import numpy as np
import jax
import jax.numpy as jnp
from jax import lax
from jax.experimental import pallas as pl
from jax.experimental.pallas import tpu as pltpu

F32 = jnp.float32
BF16 = jnp.bfloat16

EPS = 1e-6
GRID_W = 64
N_HEADS = 8
HEAD_DIM = 64
NA_KH = 8
NA_KW = 16
POOL_WINDOWS = (2, 4, 8, 16)
POOL_GROUP_DIM = 128
N_MOD = 6

LANES = 128
HALO = 8
NA_RB = 2
NA_ROWS_PER_STEP = 8
VMEM_LIMIT = 56 * 1024 * 1024

_NT = (((1,), (1,)), ((), ()))


def _cparams(n_axes):
    return pltpu.CompilerParams(dimension_semantics=("parallel",) * n_axes,
                                vmem_limit_bytes=VMEM_LIMIT)


def _resident(shape):
    nd = len(shape)
    return pl.BlockSpec(shape, lambda *_: (0,) * nd, pipeline_mode=pl.Buffered(1))


def _rms(x, g):
    return x * lax.rsqrt(jnp.mean(x * x, axis=-1, keepdims=True) + EPS) * g


def _mod_kernel(cc_ref, w_ref, b_ref, o_ref):
    a = cc_ref[...]
    a = a / (1.0 + jnp.exp(-a))
    o_ref[0] = jnp.dot(a, w_ref[0], preferred_element_type=F32,
                       precision=lax.Precision.HIGHEST) + b_ref[0]


def _modulation(cc, w_mod, b_mod):
    depth, d, n = w_mod.shape
    rows = cc.shape[0]
    tn = 1536
    return pl.pallas_call(
        _mod_kernel,
        grid=(depth, n // tn),
        in_specs=[pl.BlockSpec((rows, d), lambda l, j: (0, 0)),
                  pl.BlockSpec((1, d, tn), lambda l, j: (l, 0, j)),
                  pl.BlockSpec((1, 1, tn), lambda l, j: (l, 0, j))],
        out_specs=pl.BlockSpec((1, rows, tn), lambda l, j: (l, 0, j)),
        out_shape=jax.ShapeDtypeStruct((depth, rows, n), F32),
        compiler_params=_cparams(2),
        name="adaln_mod",
    )(cc, w_mod, b_mod.reshape(depth, 1, n))


def _inproj_kernel(x_ref, m_ref, g_ref, w_ref, u_ref, q_ref, k_ref, v_ref):
    pw = u_ref.shape[-1]
    aw = q_ref.shape[-1]
    h = _rms(x_ref[0], g_ref[...]) * (1.0 + m_ref[0, 1:2, :]) + m_ref[0, 0:1, :]
    h = h.astype(BF16)
    u_ref[0] = jnp.dot(h, w_ref[:, :pw], preferred_element_type=F32)
    q = jnp.dot(h, w_ref[:, pw:pw + aw], preferred_element_type=F32)
    q_ref[0] = (q * (HEAD_DIM ** -0.5)).astype(BF16)
    k_ref[0] = jnp.dot(h, w_ref[:, pw + aw:pw + 2 * aw], preferred_element_type=F32).astype(BF16)
    v_ref[0] = jnp.dot(h, w_ref[:, pw + 2 * aw:], preferred_element_type=F32).astype(BF16)


def _in_projection(x, mods, g, w_in, pool_width, tm):
    b, l, d = x.shape
    aw = (w_in.shape[1] - pool_width) // 3
    tok = lambda w: pl.BlockSpec((1, tm, w), lambda i, j: (i, j, 0))
    return pl.pallas_call(
        _inproj_kernel,
        grid=(b, l // tm),
        in_specs=[tok(d),
                  pl.BlockSpec((1, N_MOD, d), lambda i, j: (i, 0, 0)),
                  _resident((1, d)),
                  _resident(w_in.shape)],
        out_specs=[tok(pool_width), tok(aw), tok(aw), tok(aw)],
        out_shape=[jax.ShapeDtypeStruct((b, l, pool_width), F32)]
        + [jax.ShapeDtypeStruct((b, l, aw), BF16)] * 3,
        compiler_params=_cparams(2),
        name="in_proj",
    )(x, mods, g.reshape(1, d), w_in)


def _na_structure(rows):
    kh = min(NA_KH, rows)
    kr = NA_RB + kh - 1
    assert rows % NA_RB == 0 and rows >= kr and GRID_W >= NA_KW
    col = np.arange(GRID_W)
    q_start = np.clip(col - NA_KW // 2, 0, GRID_W - NA_KW)
    col_ok = (col[None, :] >= q_start[:, None]) & (col[None, :] < q_start[:, None] + NA_KW)
    dc = np.clip(col[None, :] - col[:, None] + NA_KW - 1, 0, 2 * NA_KW - 2)
    types, info = {}, np.zeros((2, rows // NA_RB), np.int32)
    for p in range(rows // NA_RB):
        r = p * NA_RB + np.arange(NA_RB)
        rs = np.clip(r - kh // 2, 0, rows - kh)
        rs0 = int(np.clip(rs.min(), 0, rows - kr))
        assert np.all(rs >= rs0) and np.all(rs + kh <= rs0 + kr)
        key = (tuple(rs - rs0), tuple(rs - r))
        info[0, p] = types.setdefault(key, len(types))
        info[1, p] = rs0
    nq, nk = NA_RB * GRID_W, kr * GRID_W
    dr_idx = np.zeros((len(types), nq, nk), np.int32)
    dc_idx = np.zeros_like(dr_idx)
    valid = np.zeros(dr_idx.shape, bool)
    for (off, rel), t in types.items():
        for tq in range(NA_RB):
            j = np.arange(kr)
            row_ok = (j >= off[tq]) & (j < off[tq] + kh)
            dr = np.clip(rel[tq] + (j - off[tq]) + NA_KH - 1, 0, 2 * NA_KH - 2)
            qs = slice(tq * GRID_W, (tq + 1) * GRID_W)
            valid[t, qs] = (row_ok[None, :, None] & col_ok[:, None, :]).reshape(GRID_W, nk)
            dr_idx[t, qs] = np.broadcast_to(dr[None, :, None], (GRID_W, kr, GRID_W)).reshape(GRID_W, nk)
            dc_idx[t, qs] = np.broadcast_to(dc[:, None, :], (GRID_W, kr, GRID_W)).reshape(GRID_W, nk)
    return info, dr_idx, dc_idx, valid


def _na_bias_table(rpb_l, dr_idx, dc_idx, valid):
    t = jnp.where(valid[None], rpb_l[:, dr_idx, dc_idx], -jnp.inf)
    h, nt, nq, nk = t.shape
    return t.reshape(h // 2, 2, nt, nq, nk).transpose(0, 2, 1, 3, 4).reshape(h // 2, nt, 2 * nq, nk)


def _pair_masks(n):
    lane = lax.broadcasted_iota(jnp.int32, (n, LANES), 1)
    first = lane < HEAD_DIM
    return first, first.astype(F32).astype(BF16), (~first).astype(F32).astype(BF16)


def _softmax_pv(parts):
    m = functools.reduce(jnp.maximum, [jnp.max(s, axis=1, keepdims=True) for s, _ in parts])
    es = [jnp.exp(s - m) for s, _ in parts]
    den = functools.reduce(jnp.add, [jnp.sum(e, axis=1, keepdims=True) for e in es])
    o = functools.reduce(jnp.add, [jnp.dot(e.astype(BF16), v, preferred_element_type=F32)
                                   for e, (_, v) in zip(es, parts)])
    return o / den


def _natten_kernel(info_ref, q_ref, k_ref, v_ref, kc_ref, vc_ref, tab_ref, o_ref):
    nq = NA_RB * GRID_W
    nk = tab_ref.shape[-1]
    iters = q_ref.shape[1] // nq
    first, mask_a, mask_b = _pair_masks(nq)
    g = pl.program_id(1)

    def body(j, carry):
        p = g * iters + j
        typ = info_ref[0, p]
        qoff = pl.multiple_of(j * nq, nq)
        koff = pl.multiple_of(info_ref[1, p] * GRID_W, GRID_W)
        for hp in range(N_HEADS // 2):
            ls = slice(hp * LANES, (hp + 1) * LANES)
            q2 = q_ref[0, pl.ds(qoff, nq), ls]
            qq = jnp.concatenate([q2 * mask_a, q2 * mask_b], axis=0)
            s_loc = lax.dot_general(qq, k_ref[0, pl.ds(koff, nk), ls], _NT,
                                    preferred_element_type=F32) + tab_ref[hp, typ]
            s_ctx = lax.dot_general(qq, kc_ref[0, :, ls], _NT, preferred_element_type=F32)
            o = _softmax_pv([(s_loc, v_ref[0, pl.ds(koff, nk), ls]), (s_ctx, vc_ref[0, :, ls])])
            o_ref[0, pl.ds(qoff, nq), ls] = jnp.where(first, o[:nq], o[nq:]).astype(BF16)
        return carry

    lax.fori_loop(0, iters, body, 0)


def _neighbourhood_attention(q, k, v, kc, vc, table, info):
    b, l, aw = q.shape
    c = kc.shape[1]
    tq = NA_ROWS_PER_STEP * GRID_W
    grid_spec = pltpu.PrefetchScalarGridSpec(
        num_scalar_prefetch=1,
        grid=(b, l // tq),
        in_specs=[pl.BlockSpec((1, tq, aw), lambda i, j, s: (i, j, 0)),
                  pl.BlockSpec((1, l, aw), lambda i, j, s: (i, 0, 0)),
                  pl.BlockSpec((1, l, aw), lambda i, j, s: (i, 0, 0)),
                  pl.BlockSpec((1, c, aw), lambda i, j, s: (i, 0, 0)),
                  pl.BlockSpec((1, c, aw), lambda i, j, s: (i, 0, 0)),
                  pl.BlockSpec(table.shape, lambda i, j, s: (0, 0, 0, 0), pipeline_mode=pl.Buffered(1))],
        out_specs=pl.BlockSpec((1, tq, aw), lambda i, j, s: (i, j, 0)))
    return pl.pallas_call(
        _natten_kernel,
        grid_spec=grid_spec,
        out_shape=jax.ShapeDtypeStruct((b, l, aw), BF16),
        compiler_params=_cparams(2),
        name="natten",
    )(info, q, k, v, kc, vc, table)


def _ctx_attn_kernel(q_ref, k_ref, v_ref, o_ref):
    n = q_ref.shape[1]
    first, mask_a, mask_b = _pair_masks(n)
    for hp in range(N_HEADS // 2):
        ls = slice(hp * LANES, (hp + 1) * LANES)
        q2 = q_ref[0, :, ls]
        qq = jnp.concatenate([q2 * mask_a, q2 * mask_b], axis=0)
        s = lax.dot_general(qq, k_ref[0, :, ls], _NT, preferred_element_type=F32)
        o = _softmax_pv([(s, v_ref[0, :, ls])])
        o_ref[0, :, ls] = jnp.where(first, o[:n], o[n:]).astype(BF16)


def _context_attention(q, k, v):
    b, c, aw = q.shape
    spec = pl.BlockSpec((1, c, aw), lambda i: (i, 0, 0))
    return pl.pallas_call(
        _ctx_attn_kernel,
        grid=(b,),
        in_specs=[spec, spec, spec],
        out_specs=spec,
        out_shape=jax.ShapeDtypeStruct((b, c, aw), BF16),
        compiler_params=_cparams(1),
        name="ctx_attn",
    )(q, k, v)


def _mix_out_kernel(u_ref, up_ref, un_ref, a_ref, x_ref, m_ref, g_ref, pw_ref, ps_ref, wo_ref, o_ref, *, seq):
    tm, pwidth = u_ref.shape[1], u_ref.shape[2]
    i = pl.program_id(1)
    prev = jnp.where(i > 0, up_ref[0], 0.0)
    nxt = jnp.where(i < pl.num_programs(1) - 1, un_ref[0], 0.0)
    ext = jnp.concatenate([prev, u_ref[0], nxt], axis=0)
    t = i * tm + lax.broadcasted_iota(jnp.int32, (tm, 1), 0)
    ys = []
    for gi, w in enumerate(POOL_WINDOWS):
        lo = w // 2
        hi = w - 1 - lo
        eg = ext[:, gi * POOL_GROUP_DIM:(gi + 1) * POOL_GROUP_DIM]
        tot = functools.reduce(jnp.add, [eg[HALO + o:HALO + o + tm] for o in range(-lo, hi + 1)])
        cnt = (jnp.minimum(t + hi, seq - 1) - jnp.maximum(t - lo, 0) + 1).astype(F32)
        d = tot / cnt - eg[HALO:HALO + tm]
        ys.append(jnp.dot(d.astype(BF16), pw_ref[gi], preferred_element_type=F32))
    y = (jnp.concatenate(ys, axis=1) * ps_ref[...]).astype(BF16)
    mix = (jnp.dot(y, wo_ref[:pwidth], preferred_element_type=F32)
           + jnp.dot(a_ref[0], wo_ref[pwidth:], preferred_element_type=F32))
    o_ref[0] = x_ref[0] + m_ref[0, 2:3, :] * _rms(mix, g_ref[...])


def _mix_out(u, attn, x, mods, g, pool_w, pool_scale, w_out, tm):
    b, l, d = x.shape
    pwidth, aw = u.shape[-1], attn.shape[-1]
    assert max(POOL_WINDOWS) // 2 <= HALO and tm % HALO == 0
    nh = tm // HALO
    last = l // HALO - 1
    tok = lambda w: pl.BlockSpec((1, tm, w), lambda i, j: (i, j, 0))
    return pl.pallas_call(
        functools.partial(_mix_out_kernel, seq=l),
        grid=(b, l // tm),
        in_specs=[tok(pwidth),
                  pl.BlockSpec((1, HALO, pwidth), lambda i, j: (i, jnp.maximum(j * nh - 1, 0), 0)),
                  pl.BlockSpec((1, HALO, pwidth), lambda i, j: (i, jnp.minimum((j + 1) * nh, last), 0)),
                  tok(aw), tok(d),
                  pl.BlockSpec((1, N_MOD, d), lambda i, j: (i, 0, 0)),
                  _resident((1, d)),
                  _resident(pool_w.shape),
                  _resident((1, pwidth)),
                  _resident(w_out.shape)],
        out_specs=tok(d),
        out_shape=jax.ShapeDtypeStruct((b, l, d), F32),
        compiler_params=_cparams(2),
        name="mix_out",
    )(u, u, u, attn, x, mods, g.reshape(1, d), pool_w, pool_scale.reshape(1, pwidth), w_out)


def _mlp_kernel(x_ref, m_ref, g1_ref, g2_ref, w1_ref, w2_ref, o_ref, acc_ref, *, tf):
    x = x_ref[0]
    h = (_rms(x, g1_ref[...]) * (1.0 + m_ref[0, 4:5, :]) + m_ref[0, 3:4, :]).astype(BF16)
    for c in range(w1_ref.shape[1] // tf):
        a = jnp.maximum(jnp.dot(h, w1_ref[:, c * tf:(c + 1) * tf], preferred_element_type=F32), 0.0)
        part = jnp.dot((a * a).astype(BF16), w2_ref[c * tf:(c + 1) * tf, :], preferred_element_type=F32)
        if c == 0:
            acc_ref[...] = part
        else:
            acc_ref[...] += part
    o_ref[0] = x + m_ref[0, 5:6, :] * _rms(acc_ref[...], g2_ref[...])


def _mlp(x, mods, g1, g2, w1, w2, tm, tf=1024):
    b, l, d = x.shape
    tok = pl.BlockSpec((1, tm, d), lambda i, j: (i, j, 0))
    return pl.pallas_call(
        functools.partial(_mlp_kernel, tf=tf),
        grid=(b, l // tm),
        in_specs=[tok,
                  pl.BlockSpec((1, N_MOD, d), lambda i, j: (i, 0, 0)),
                  _resident((1, d)), _resident((1, d)),
                  _resident(w1.shape), _resident(w2.shape)],
        out_specs=tok,
        out_shape=jax.ShapeDtypeStruct((b, l, d), F32),
        scratch_shapes=[pltpu.VMEM((tm, d), F32)],
        compiler_params=_cparams(2),
        name="mlp",
    )(x, mods, g1.reshape(1, d), g2.reshape(1, d), w1, w2)


def kernel(x, c, ctx, c_ctx, w_mod, b_mod, g_pre_mix, g_post_mix, g_pre_mlp, g_post_mlp,
           w_in, pool_w, pool_scale, rpb, w_out, w_mlp_in, w_mlp_out):
    b, l, d = x.shape
    depth = w_mod.shape[0]
    n_ctx = ctx.shape[1]
    pool_width = pool_scale.shape[-1]
    rows = l // GRID_W
    tm_x = min(1024, l)

    mod_rows = -(-(b + 1) // 8) * 8
    cc = jnp.zeros((mod_rows, d), F32).at[:b].set(c).at[b].set(c_ctx)
    mods = _modulation(cc, w_mod, b_mod).reshape(depth, mod_rows, N_MOD, d)

    w_in_b, pool_w_b, w_out_b = w_in.astype(BF16), pool_w.astype(BF16), w_out.astype(BF16)
    w1_b, w2_b = w_mlp_in.astype(BF16), w_mlp_out.astype(BF16)

    info, dr_idx, dc_idx, valid = _na_structure(rows)
    info = jnp.asarray(info)

    for li in range(depth):
        mx = mods[li, :b]
        mc = jnp.broadcast_to(mods[li, b], (b, N_MOD, d))
        u, q, k, v = _in_projection(x, mx, g_pre_mix[li], w_in_b[li], pool_width, tm_x)
        uc, qc, kc, vc = _in_projection(ctx, mc, g_pre_mix[li], w_in_b[li], pool_width, n_ctx)
        table = _na_bias_table(rpb[li], dr_idx, dc_idx, valid)
        attn = _neighbourhood_attention(q, k, v, kc, vc, table, info)
        x = _mix_out(u, attn, x, mx, g_post_mix[li], pool_w_b[li], pool_scale[li], w_out_b[li], tm_x)
        x = _mlp(x, mx, g_pre_mlp[li], g_post_mlp[li], w1_b[li], w2_b[li], tm_x)
        if li != depth - 1:
            attn_c = _context_attention(qc, kc, vc)
            ctx = _mix_out(uc, attn_c, ctx, mc, g_post_mix[li], pool_w_b[li], pool_scale[li], w_out_b[li], n_ctx)
            ctx = _mlp(ctx, mc, g_pre_mlp[li], g_post_mlp[li], w1_b[li], w2_b[li], n_ctx)
    return x
```

```python
import functools

import numpy as np
import jax
import jax.numpy as jnp
from jax import lax
from jax.experimental import pallas as pl
from jax.experimental.pallas import tpu as pltpu

F32 = jnp.float32
BF16 = jnp.bfloat16

EPS = 1e-6
GRID_W = 64
N_HEADS = 8
HEAD_DIM = 64
NA_KH = 8
NA_KW = 16
POOL_WINDOWS = (2, 4, 8, 16)
POOL_GROUP_DIM = 128
N_MOD = 6

LANES = 128
HALO = 8
NA_RB = 2
NA_ROWS_PER_STEP = 8
VMEM_LIMIT = 56 * 1024 * 1024

_NT = (((1,), (1,)), ((), ()))


def _cparams(n_axes):
    return pltpu.CompilerParams(dimension_semantics=("parallel",) * n_axes,
                                vmem_limit_bytes=VMEM_LIMIT)


def _resident(shape):
    nd = len(shape)
    return pl.BlockSpec(shape, lambda *_: (0,) * nd, pipeline_mode=pl.Buffered(1))


def _rms(x, g):
    return x * lax.rsqrt(jnp.mean(x * x, axis=-1, keepdims=True) + EPS) * g


def _mod_kernel(cc_ref, w_ref, b_ref, o_ref):
    a = cc_ref[...]
    a = a / (1.0 + jnp.exp(-a))
    o_ref[0] = jnp.dot(a, w_ref[0], preferred_element_type=F32,
                       precision=lax.Precision.HIGHEST) + b_ref[0]


def _modulation(cc, w_mod, b_mod):
    depth, d, n = w_mod.shape
    rows = cc.shape[0]
    tn = 1536
    return pl.pallas_call(
        _mod_kernel,
        grid=(depth, n // tn),
        in_specs=[pl.BlockSpec((rows, d), lambda l, j: (0, 0)),
                  pl.BlockSpec((1, d, tn), lambda l, j: (l, 0, j)),
                  pl.BlockSpec((1, 1, tn), lambda l, j: (l, 0, j))],
        out_specs=pl.BlockSpec((1, rows, tn), lambda l, j: (l, 0, j)),
        out_shape=jax.ShapeDtypeStruct((depth, rows, n), F32),
        compiler_params=_cparams(2),
        name="adaln_mod",
    )(cc, w_mod, b_mod.reshape(depth, 1, n))


def _inproj_kernel(x_ref, m_ref, g_ref, w_ref, u_ref, q_ref, k_ref, v_ref):
    pw = u_ref.shape[-1]
    aw = q_ref.shape[-1]
    h = _rms(x_ref[0], g_ref[...]) * (1.0 + m_ref[0, 1:2, :]) + m_ref[0, 0:1, :]
    h = h.astype(BF16)
    u_ref[0] = jnp.dot(h, w_ref[:, :pw], preferred_element_type=F32)
    q = jnp.dot(h, w_ref[:, pw:pw + aw], preferred_element_type=F32)
    q_ref[0] = (q * (HEAD_DIM ** -0.5)).astype(BF16)
    k_ref[0] = jnp.dot(h, w_ref[:, pw + aw:pw + 2 * aw], preferred_element_type=F32).astype(BF16)
    v_ref[0] = jnp.dot(h, w_ref[:, pw + 2 * aw:], preferred_element_type=F32).astype(BF16)


def _in_projection(x, mods, g, w_in, pool_width, tm):
    b, l, d = x.shape
    aw = (w_in.shape[1] - pool_width) // 3
    tok = lambda w: pl.BlockSpec((1, tm, w), lambda i, j: (i, j, 0))
    return pl.pallas_call(
        _inproj_kernel,
        grid=(b, l // tm),
        in_specs=[tok(d),
                  pl.BlockSpec((1, N_MOD, d), lambda i, j: (i, 0, 0)),
                  _resident((1, d)),
                  _resident(w_in.shape)],
        out_specs=[tok(pool_width), tok(aw), tok(aw), tok(aw)],
        out_shape=[jax.ShapeDtypeStruct((b, l, pool_width), F32)]
        + [jax.ShapeDtypeStruct((b, l, aw), BF16)] * 3,
        compiler_params=_cparams(2),
        name="in_proj",
    )(x, mods, g.reshape(1, d), w_in)


def _na_structure(rows):
    kh = min(NA_KH, rows)
    kr = NA_RB + kh - 1
    assert rows % NA_RB == 0 and rows >= kr and GRID_W >= NA_KW
    types, info = {}, np.zeros((2, rows // NA_RB), np.int32)
    for p in range(rows // NA_RB):
        r = p * NA_RB + np.arange(NA_RB)
        rs = np.clip(r - kh // 2, 0, rows - kh)
        rs0 = int(np.clip(rs.min(), 0, rows - kr))
        assert np.all(rs >= rs0) and np.all(rs + kh <= rs0 + kr)
        key = (tuple(int(a) for a in rs - rs0), tuple(int(a) for a in rs - r))
        info[0, p] = types.setdefault(key, len(types))
        info[1, p] = rs0
    return info, types, kh, kr


def _bias_table_kernel(rpb_ref, o_ref, *, types, kh, kr):
    n_dr, n_dc = 2 * NA_KH - 1, 2 * NA_KW - 1
    cq = lax.broadcasted_iota(jnp.int32, (GRID_W, GRID_W), 0)
    kc = lax.broadcasted_iota(jnp.int32, (GRID_W, GRID_W), 1)
    q_start = jnp.clip(cq - NA_KW // 2, 0, GRID_W - NA_KW)
    col_ok = (kc >= q_start) & (kc < q_start + NA_KW)
    dc = kc - cq + (NA_KW - 1)
    neg = jnp.full((GRID_W, GRID_W), -jnp.inf, F32)
    for s in range(2):
        h = pl.program_id(0) * 2 + s
        tiles = []
        for dr in range(n_dr):
            e = neg
            for j in range(n_dc):
                e = jnp.where(dc == j, rpb_ref[(h * n_dr + dr) * n_dc + j], e)
            tiles.append(jnp.where(col_ok, e, neg))
        for (off, rel), t in types.items():
            for tq in range(NA_RB):
                blocks = [tiles[rel[tq] + (j - off[tq]) + NA_KH - 1] if off[tq] <= j < off[tq] + kh else neg
                          for j in range(kr)]
                row0 = (s * NA_RB + tq) * GRID_W
                o_ref[0, t, row0:row0 + GRID_W, :] = jnp.concatenate(blocks, axis=1)


def _na_bias_table(rpb_l, types, kh, kr):
    h = rpb_l.shape[0]
    shape = (h // 2, len(types), 2 * NA_RB * GRID_W, kr * GRID_W)
    return pl.pallas_call(
        functools.partial(_bias_table_kernel, types=types, kh=kh, kr=kr),
        grid=(h // 2,),
        in_specs=[pl.BlockSpec(memory_space=pltpu.SMEM)],
        out_specs=pl.BlockSpec((1,) + shape[1:], lambda i: (i, 0, 0, 0)),
        out_shape=jax.ShapeDtypeStruct(shape, F32),
        compiler_params=_cparams(1),
        name="na_bias_table",
    )(rpb_l.reshape(-1))


def _pair_masks(n):
    lane = lax.broadcasted_iota(jnp.int32, (n, LANES), 1)
    first = lane < HEAD_DIM
    return first, first.astype(F32).astype(BF16), (~first).astype(F32).astype(BF16)


def _softmax_pv(parts):
    m = functools.reduce(jnp.maximum, [jnp.max(s, axis=1, keepdims=True) for s, _ in parts])
    es = [jnp.exp(s - m) for s, _ in parts]
    den = functools.reduce(jnp.add, [jnp.sum(e, axis=1, keepdims=True) for e in es])
    o = functools.reduce(jnp.add, [jnp.dot(e.astype(BF16), v, preferred_element_type=F32)
                                   for e, (_, v) in zip(es, parts)])
    return o / den


def _natten_kernel(info_ref, q_ref, k_ref, v_ref, kc_ref, vc_ref, tab_ref, o_ref):
    nq = NA_RB * GRID_W
    nk = tab_ref.shape[-1]
    iters = q_ref.shape[1] // nq
    first, mask_a, mask_b = _pair_masks(nq)
    g = pl.program_id(1)

    def body(j, carry):
        p = g * iters + j
        typ = info_ref[0, p]
        qoff = pl.multiple_of(j * nq, nq)
        koff = pl.multiple_of(info_ref[1, p] * GRID_W, GRID_W)
        for hp in range(N_HEADS // 2):
            ls = slice(hp * LANES, (hp + 1) * LANES)
            q2 = q_ref[0, pl.ds(qoff, nq), ls]
            qq = jnp.concatenate([q2 * mask_a, q2 * mask_b], axis=0)
            s_loc = lax.dot_general(qq, k_ref[0, pl.ds(koff, nk), ls], _NT,
                                    preferred_element_type=F32) + tab_ref[hp, typ]
            s_ctx = lax.dot_general(qq, kc_ref[0, :, ls], _NT, preferred_element_type=F32)
            o = _softmax_pv([(s_loc, v_ref[0, pl.ds(koff, nk), ls]), (s_ctx, vc_ref[0, :, ls])])
            o_ref[0, pl.ds(qoff, nq), ls] = jnp.where(first, o[:nq], o[nq:]).astype(BF16)
        return carry

    lax.fori_loop(0, iters, body, 0)


def _neighbourhood_attention(q, k, v, kc, vc, table, info):
    b, l, aw = q.shape
    c = kc.shape[1]
    tq = NA_ROWS_PER_STEP * GRID_W
    grid_spec = pltpu.PrefetchScalarGridSpec(
        num_scalar_prefetch=1,
        grid=(b, l // tq),
        in_specs=[pl.BlockSpec((1, tq, aw), lambda i, j, s: (i, j, 0)),
                  pl.BlockSpec((1, l, aw), lambda i, j, s: (i, 0, 0)),
                  pl.BlockSpec((1, l, aw), lambda i, j, s: (i, 0, 0)),
                  pl.BlockSpec((1, c, aw), lambda i, j, s: (i, 0, 0)),
                  pl.BlockSpec((1, c, aw), lambda i, j, s: (i, 0, 0)),
                  pl.BlockSpec(table.shape, lambda i, j, s: (0, 0, 0, 0), pipeline_mode=pl.Buffered(1))],
        out_specs=pl.BlockSpec((1, tq, aw), lambda i, j, s: (i, j, 0)))
    return pl.pallas_call(
        _natten_kernel,
        grid_spec=grid_spec,
        out_shape=jax.ShapeDtypeStruct((b, l, aw), BF16),
        compiler_params=_cparams(2),
        name="natten",
    )(info, q, k, v, kc, vc, table)


def _ctx_attn_kernel(q_ref, k_ref, v_ref, o_ref):
    n = q_ref.shape[1]
    first, mask_a, mask_b = _pair_masks(n)
    for hp in range(N_HEADS // 2):
        ls = slice(hp * LANES, (hp + 1) * LANES)
        q2 = q_ref[0, :, ls]
        qq = jnp.concatenate([q2 * mask_a, q2 * mask_b], axis=0)
        s = lax.dot_general(qq, k_ref[0, :, ls], _NT, preferred_element_type=F32)
        o = _softmax_pv([(s, v_ref[0, :, ls])])
        o_ref[0, :, ls] = jnp.where(first, o[:n], o[n:]).astype(BF16)


def _context_attention(q, k, v):
    b, c, aw = q.shape
    spec = pl.BlockSpec((1, c, aw), lambda i: (i, 0, 0))
    return pl.pallas_call(
        _ctx_attn_kernel,
        grid=(b,),
        in_specs=[spec, spec, spec],
        out_specs=spec,
        out_shape=jax.ShapeDtypeStruct((b, c, aw), BF16),
        compiler_params=_cparams(1),
        name="ctx_attn",
    )(q, k, v)


def _mix_out_kernel(u_ref, up_ref, un_ref, a_ref, x_ref, m_ref, g_ref, pw_ref, ps_ref, wo_ref, o_ref, *, seq):
    tm, pwidth = u_ref.shape[1], u_ref.shape[2]
    i = pl.program_id(1)
    prev = jnp.where(i > 0, up_ref[0], 0.0)
    nxt = jnp.where(i < pl.num_programs(1) - 1, un_ref[0], 0.0)
    ext = jnp.concatenate([prev, u_ref[0], nxt], axis=0)
    t = i * tm + lax.broadcasted_iota(jnp.int32, (tm, 1), 0)
    ys = []
    for gi, w in enumerate(POOL_WINDOWS):
        lo = w // 2
        hi = w - 1 - lo
        eg = ext[:, gi * POOL_GROUP_DIM:(gi + 1) * POOL_GROUP_DIM]
        tot = functools.reduce(jnp.add, [eg[HALO + o:HALO + o + tm] for o in range(-lo, hi + 1)])
        cnt = (jnp.minimum(t + hi, seq - 1) - jnp.maximum(t - lo, 0) + 1).astype(F32)
        d = tot / cnt - eg[HALO:HALO + tm]
        ys.append(jnp.dot(d.astype(BF16), pw_ref[gi], preferred_element_type=F32))
    y = (jnp.concatenate(ys, axis=1) * ps_ref[...]).astype(BF16)
    mix = (jnp.dot(y, wo_ref[:pwidth], preferred_element_type=F32)
           + jnp.dot(a_ref[0], wo_ref[pwidth:], preferred_element_type=F32))
    o_ref[0] = x_ref[0] + m_ref[0, 2:3, :] * _rms(mix, g_ref[...])


def _mix_out(u, attn, x, mods, g, pool_w, pool_scale, w_out, tm):
    b, l, d = x.shape
    pwidth, aw = u.shape[-1], attn.shape[-1]
    assert max(POOL_WINDOWS) // 2 <= HALO and tm % HALO == 0
    nh = tm // HALO
    last = l // HALO - 1
    tok = lambda w: pl.BlockSpec((1, tm, w), lambda i, j: (i, j, 0))
    return pl.pallas_call(
        functools.partial(_mix_out_kernel, seq=l),
        grid=(b, l // tm),
        in_specs=[tok(pwidth),
                  pl.BlockSpec((1, HALO, pwidth), lambda i, j: (i, jnp.maximum(j * nh - 1, 0), 0)),
                  pl.BlockSpec((1, HALO, pwidth), lambda i, j: (i, jnp.minimum((j + 1) * nh, last), 0)),
                  tok(aw), tok(d),
                  pl.BlockSpec((1, N_MOD, d), lambda i, j: (i, 0, 0)),
                  _resident((1, d)),
                  _resident(pool_w.shape),
                  _resident((1, pwidth)),
                  _resident(w_out.shape)],
        out_specs=tok(d),
        out_shape=jax.ShapeDtypeStruct((b, l, d), F32),
        compiler_params=_cparams(2),
        name="mix_out",
    )(u, u, u, attn, x, mods, g.reshape(1, d), pool_w, pool_scale.reshape(1, pwidth), w_out)


def _mlp_kernel(x_ref, m_ref, g1_ref, g2_ref, w1_ref, w2_ref, o_ref, acc_ref, *, tf):
    x = x_ref[0]
    h = (_rms(x, g1_ref[...]) * (1.0 + m_ref[0, 4:5, :]) + m_ref[0, 3:4, :]).astype(BF16)
    for c in range(w1_ref.shape[1] // tf):
        a = jnp.maximum(jnp.dot(h, w1_ref[:, c * tf:(c + 1) * tf], preferred_element_type=F32), 0.0)
        part = jnp.dot((a * a).astype(BF16), w2_ref[c * tf:(c + 1) * tf, :], preferred_element_type=F32)
        if c == 0:
            acc_ref[...] = part
        else:
            acc_ref[...] += part
    o_ref[0] = x + m_ref[0, 5:6, :] * _rms(acc_ref[...], g2_ref[...])


def _mlp(x, mods, g1, g2, w1, w2, tm, tf=1024):
    b, l, d = x.shape
    tok = pl.BlockSpec((1, tm, d), lambda i, j: (i, j, 0))
    return pl.pallas_call(
        functools.partial(_mlp_kernel, tf=tf),
        grid=(b, l // tm),
        in_specs=[tok,
                  pl.BlockSpec((1, N_MOD, d), lambda i, j: (i, 0, 0)),
                  _resident((1, d)), _resident((1, d)),
                  _resident(w1.shape), _resident(w2.shape)],
        out_specs=tok,
        out_shape=jax.ShapeDtypeStruct((b, l, d), F32),
        scratch_shapes=[pltpu.VMEM((tm, d), F32)],
        compiler_params=_cparams(2),
        name="mlp",
    )(x, mods, g1.reshape(1, d), g2.reshape(1, d), w1, w2)


def kernel(x, c, ctx, c_ctx, w_mod, b_mod, g_pre_mix, g_post_mix, g_pre_mlp, g_post_mlp,
           w_in, pool_w, pool_scale, rpb, w_out, w_mlp_in, w_mlp_out):
    b, l, d = x.shape
    depth = w_mod.shape[0]
    n_ctx = ctx.shape[1]
    pool_width = pool_scale.shape[-1]
    rows = l // GRID_W
    tm_x = min(1024, l)

    mod_rows = -(-(b + 1) // 8) * 8
    cc = jnp.zeros((mod_rows, d), F32).at[:b].set(c).at[b].set(c_ctx)
    mods = _modulation(cc, w_mod, b_mod).reshape(depth, mod_rows, N_MOD, d)

    w_in_b, pool_w_b, w_out_b = w_in.astype(BF16), pool_w.astype(BF16), w_out.astype(BF16)
    w1_b, w2_b = w_mlp_in.astype(BF16), w_mlp_out.astype(BF16)

    info, types, kh, kr = _na_structure(rows)
    info = jnp.asarray(info)

    for li in range(depth):
        mx = mods[li, :b]
        mc = jnp.broadcast_to(mods[li, b], (b, N_MOD, d))
        u, q, k, v = _in_projection(x, mx, g_pre_mix[li], w_in_b[li], pool_width, tm_x)
        uc, qc, kc, vc = _in_projection(ctx, mc, g_pre_mix[li], w_in_b[li], pool_width, n_ctx)
        table = _na_bias_table(rpb[li], types, kh, kr)
        attn = _neighbourhood_attention(q, k, v, kc, vc, table, info)
        x = _mix_out(u, attn, x, mx, g_post_mix[li], pool_w_b[li], pool_scale[li], w_out_b[li], tm_x)
        x = _mlp(x, mx, g_pre_mlp[li], g_post_mlp[li], w1_b[li], w2_b[li], tm_x)
        if li != depth - 1:
            attn_c = _context_attention(qc, kc, vc)
            ctx = _mix_out(uc, attn_c, ctx, mc, g_post_mix[li], pool_w_b[li], pool_scale[li], w_out_b[li], n_ctx)
            ctx = _mlp(ctx, mc, g_pre_mlp[li], g_post_mlp[li], w1_b[li], w2_b[li], n_ctx)
    return x
```

```python
import functools

import numpy as np
import jax
import jax.numpy as jnp
from jax import lax
from jax.experimental import pallas as pl
from jax.experimental.pallas import tpu as pltpu

F32 = jnp.float32
BF16 = jnp.bfloat16

EPS = 1e-6
GRID_W = 64
N_HEADS = 8
HEAD_DIM = 64
NA_KH = 8
NA_KW = 16
POOL_WINDOWS = (2, 4, 8, 16)
POOL_GROUP_DIM = 128
N_MOD = 6

LANES = 128
HALO = 8
NA_RB = 2
NA_ROWS_PER_STEP = 8
VMEM_LIMIT = 56 * 1024 * 1024

_NT = (((1,), (1,)), ((), ()))


def _cparams(n_axes):
    return pltpu.CompilerParams(dimension_semantics=("parallel",) * n_axes,
                                vmem_limit_bytes=VMEM_LIMIT)


def _resident(shape):
    nd = len(shape)
    return pl.BlockSpec(shape, lambda *_: (0,) * nd, pipeline_mode=pl.Buffered(1))


def _rms(x, g):
    return x * lax.rsqrt(jnp.mean(x * x, axis=-1, keepdims=True) + EPS) * g


def _mod_kernel(cc_ref, w_ref, b_ref, o_ref):
    a = cc_ref[...]
    a = a / (1.0 + jnp.exp(-a))
    o_ref[0] = jnp.dot(a, w_ref[0], preferred_element_type=F32,
                       precision=lax.Precision.HIGHEST) + b_ref[0]


def _modulation(cc, w_mod, b_mod):
    depth, d, n = w_mod.shape
    rows = cc.shape[0]
    tn = 1536
    return pl.pallas_call(
        _mod_kernel,
        grid=(depth, n // tn),
        in_specs=[pl.BlockSpec((rows, d), lambda l, j: (0, 0)),
                  pl.BlockSpec((1, d, tn), lambda l, j: (l, 0, j)),
                  pl.BlockSpec((1, 1, tn), lambda l, j: (l, 0, j))],
        out_specs=pl.BlockSpec((1, rows, tn), lambda l, j: (l, 0, j)),
        out_shape=jax.ShapeDtypeStruct((depth, rows, n), F32),
        compiler_params=_cparams(2),
        name="adaln_mod",
    )(cc, w_mod, b_mod.reshape(depth, 1, n))


def _inproj_kernel(x_ref, m_ref, g_ref, w_ref, wkt_ref, u_ref, q_ref, kt_ref, v_ref):
    pw = u_ref.shape[-1]
    aw = q_ref.shape[-1]
    h = _rms(x_ref[0], g_ref[...]) * (1.0 + m_ref[0, 1:2, :]) + m_ref[0, 0:1, :]
    h = h.astype(BF16)
    u_ref[0] = jnp.dot(h, w_ref[:, :pw], preferred_element_type=F32)
    q = jnp.dot(h, w_ref[:, pw:pw + aw], preferred_element_type=F32)
    q_ref[0] = (q * (HEAD_DIM ** -0.5)).astype(BF16)
    kt_ref[0] = lax.dot_general(wkt_ref[...], h, _NT, preferred_element_type=F32).astype(BF16)
    v_ref[0] = jnp.dot(h, w_ref[:, pw + 2 * aw:], preferred_element_type=F32).astype(BF16)


def _in_projection(x, mods, g, w_in, w_kt, pool_width, tm):
    b, l, d = x.shape
    aw = w_kt.shape[0]
    tok = lambda w: pl.BlockSpec((1, tm, w), lambda i, j: (i, j, 0))
    return pl.pallas_call(
        _inproj_kernel,
        grid=(b, l // tm),
        in_specs=[tok(d),
                  pl.BlockSpec((1, N_MOD, d), lambda i, j: (i, 0, 0)),
                  _resident((1, d)),
                  _resident(w_in.shape),
                  _resident(w_kt.shape)],
        out_specs=[tok(pool_width), tok(aw), pl.BlockSpec((1, aw, tm), lambda i, j: (i, 0, j)), tok(aw)],
        out_shape=[jax.ShapeDtypeStruct((b, l, pool_width), F32),
                   jax.ShapeDtypeStruct((b, l, aw), BF16),
                   jax.ShapeDtypeStruct((b, aw, l), BF16),
                   jax.ShapeDtypeStruct((b, l, aw), BF16)],
        compiler_params=_cparams(2),
        name="in_proj",
    )(x, mods, g.reshape(1, d), w_in, w_kt)


def _na_structure(rows):
    kh = min(NA_KH, rows)
    step = LANES // GRID_W
    kr = -(-(NA_RB + kh - 1 + step - 1) // step) * step
    assert rows % NA_RB == 0 and rows % step == 0 and rows >= kr and GRID_W >= NA_KW
    types, info = {}, np.zeros((2, rows // NA_RB), np.int32)
    for p in range(rows // NA_RB):
        r = p * NA_RB + np.arange(NA_RB)
        rs = np.clip(r - kh // 2, 0, rows - kh)
        rs0 = int(np.clip(rs.min() // step * step, 0, rows - kr))
        assert rs0 % step == 0 and np.all(rs >= rs0) and np.all(rs + kh <= rs0 + kr)
        key = (tuple(int(a) for a in rs - rs0), tuple(int(a) for a in rs - r))
        info[0, p] = types.setdefault(key, len(types))
        info[1, p] = rs0
    return info, types, kh, kr


def _bias_table_kernel(rpb_ref, o_ref, *, types, kh, kr):
    n_dr, n_dc = 2 * NA_KH - 1, 2 * NA_KW - 1
    cq = lax.broadcasted_iota(jnp.int32, (GRID_W, GRID_W), 0)
    kc = lax.broadcasted_iota(jnp.int32, (GRID_W, GRID_W), 1)
    q_start = jnp.clip(cq - NA_KW // 2, 0, GRID_W - NA_KW)
    col_ok = (kc >= q_start) & (kc < q_start + NA_KW)
    dc = kc - cq + (NA_KW - 1)
    neg = jnp.full((GRID_W, GRID_W), -jnp.inf, F32)
    for s in range(2):
        h = pl.program_id(0) * 2 + s
        tiles = []
        for dr in range(n_dr):
            e = neg
            for j in range(n_dc):
                e = jnp.where(dc == j, rpb_ref[(h * n_dr + dr) * n_dc + j], e)
            tiles.append(jnp.where(col_ok, e, neg))
        for (off, rel), t in types.items():
            for tq in range(NA_RB):
                blocks = [tiles[rel[tq] + (j - off[tq]) + NA_KH - 1] if off[tq] <= j < off[tq] + kh else neg
                          for j in range(kr)]
                row0 = (s * NA_RB + tq) * GRID_W
                o_ref[0, t, row0:row0 + GRID_W, :] = jnp.concatenate(blocks, axis=1)


def _na_bias_table(rpb_l, types, kh, kr):
    h = rpb_l.shape[0]
    shape = (h // 2, len(types), 2 * NA_RB * GRID_W, kr * GRID_W)
    return pl.pallas_call(
        functools.partial(_bias_table_kernel, types=types, kh=kh, kr=kr),
        grid=(h // 2,),
        in_specs=[pl.BlockSpec(memory_space=pltpu.SMEM)],
        out_specs=pl.BlockSpec((1,) + shape[1:], lambda i: (i, 0, 0, 0)),
        out_shape=jax.ShapeDtypeStruct(shape, F32),
        compiler_params=_cparams(1),
        name="na_bias_table",
    )(rpb_l.reshape(-1))


def _pair_masks(n):
    lane = lax.broadcasted_iota(jnp.int32, (n, LANES), 1)
    first = lane < HEAD_DIM
    return first, first.astype(F32).astype(BF16), (~first).astype(F32).astype(BF16)


def _softmax_pv(parts):
    m = functools.reduce(jnp.maximum, [jnp.max(s, axis=1, keepdims=True) for s, _ in parts])
    es = [jnp.exp(s - m) for s, _ in parts]
    den = functools.reduce(jnp.add, [jnp.sum(e, axis=1, keepdims=True) for e in es])
    o = functools.reduce(jnp.add, [jnp.dot(e.astype(BF16), v, preferred_element_type=F32)
                                   for e, (_, v) in zip(es, parts)])
    return o / den


def _natten_kernel(info_ref, q_ref, kt_ref, v_ref, kct_ref, vc_ref, tab_ref, o_ref):
    nq = NA_RB * GRID_W
    nk = tab_ref.shape[-1]
    iters = q_ref.shape[1] // nq
    first, mask_a, mask_b = _pair_masks(nq)
    g = pl.program_id(1)

    def body(j, carry):
        p = g * iters + j
        typ = info_ref[0, p]
        qoff = pl.multiple_of(j * nq, nq)
        koff = pl.multiple_of(info_ref[1, p] * GRID_W, LANES)
        def scores(hp):
            ls = slice(hp * LANES, (hp + 1) * LANES)
            q2 = q_ref[0, pl.ds(qoff, nq), ls]
            qq = jnp.concatenate([q2 * mask_a, q2 * mask_b], axis=0)
            s_loc = jnp.dot(qq, kt_ref[0, ls, pl.ds(koff, nk)],
                            preferred_element_type=F32) + tab_ref[hp, typ]
            return s_loc, jnp.dot(qq, kct_ref[0, ls, :], preferred_element_type=F32)

        n_pairs = N_HEADS // 2
        nxt = scores(0)
        for hp in range(n_pairs):
            ls = slice(hp * LANES, (hp + 1) * LANES)
            (s_loc, s_ctx), nxt = nxt, (scores(hp + 1) if hp + 1 < n_pairs else None)
            o = _softmax_pv([(s_loc, v_ref[0, pl.ds(koff, nk), ls]), (s_ctx, vc_ref[0, :, ls])])
            o_ref[0, pl.ds(qoff, nq), ls] = jnp.where(first, o[:nq], o[nq:]).astype(BF16)
        return carry

    lax.fori_loop(0, iters, body, 0)


def _neighbourhood_attention(q, kt, v, kct, vc, table, info):
    b, l, aw = q.shape
    c = vc.shape[1]
    tq = NA_ROWS_PER_STEP * GRID_W
    per_batch = lambda shape: pl.BlockSpec((1,) + shape, lambda i, j, s: (i, 0, 0))
    grid_spec = pltpu.PrefetchScalarGridSpec(
        num_scalar_prefetch=1,
        grid=(b, l // tq),
        in_specs=[pl.BlockSpec((1, tq, aw), lambda i, j, s: (i, j, 0)),
                  per_batch((aw, l)), per_batch((l, aw)), per_batch((aw, c)), per_batch((c, aw)),
                  pl.BlockSpec(table.shape, lambda i, j, s: (0, 0, 0, 0), pipeline_mode=pl.Buffered(1))],
        out_specs=pl.BlockSpec((1, tq, aw), lambda i, j, s: (i, j, 0)))
    return pl.pallas_call(
        _natten_kernel,
        grid_spec=grid_spec,
        out_shape=jax.ShapeDtypeStruct((b, l, aw), BF16),
        compiler_params=_cparams(2),
        name="natten",
    )(info, q, kt, v, kct, vc, table)


def _ctx_attn_kernel(q_ref, kt_ref, v_ref, o_ref):
    n = q_ref.shape[1]
    first, mask_a, mask_b = _pair_masks(n)
    for hp in range(N_HEADS // 2):
        ls = slice(hp * LANES, (hp + 1) * LANES)
        q2 = q_ref[0, :, ls]
        qq = jnp.concatenate([q2 * mask_a, q2 * mask_b], axis=0)
        s = jnp.dot(qq, kt_ref[0, ls, :], preferred_element_type=F32)
        o = _softmax_pv([(s, v_ref[0, :, ls])])
        o_ref[0, :, ls] = jnp.where(first, o[:n], o[n:]).astype(BF16)


def _context_attention(q, kt, v):
    b, c, aw = q.shape
    spec = pl.BlockSpec((1, c, aw), lambda i: (i, 0, 0))
    return pl.pallas_call(
        _ctx_attn_kernel,
        grid=(b,),
        in_specs=[spec, pl.BlockSpec((1, aw, c), lambda i: (i, 0, 0)), spec],
        out_specs=spec,
        out_shape=jax.ShapeDtypeStruct((b, c, aw), BF16),
        compiler_params=_cparams(1),
        name="ctx_attn",
    )(q, kt, v)


def _mix_out_kernel(u_ref, up_ref, un_ref, a_ref, x_ref, m_ref, g_ref, pw_ref, ps_ref, wo_ref, o_ref, *, seq):
    tm, pwidth = u_ref.shape[1], u_ref.shape[2]
    i = pl.program_id(1)
    prev = jnp.where(i > 0, up_ref[0], 0.0)
    nxt = jnp.where(i < pl.num_programs(1) - 1, un_ref[0], 0.0)
    ext = jnp.concatenate([prev, u_ref[0], nxt], axis=0)
    t = i * tm + lax.broadcasted_iota(jnp.int32, (tm, 1), 0)
    ys = []
    for gi, w in enumerate(POOL_WINDOWS):
        lo = w // 2
        hi = w - 1 - lo
        eg = ext[:, gi * POOL_GROUP_DIM:(gi + 1) * POOL_GROUP_DIM]
        tot = functools.reduce(jnp.add, [eg[HALO + o:HALO + o + tm] for o in range(-lo, hi + 1)])
        cnt = (jnp.minimum(t + hi, seq - 1) - jnp.maximum(t - lo, 0) + 1).astype(F32)
        d = tot / cnt - eg[HALO:HALO + tm]
        ys.append(jnp.dot(d.astype(BF16), pw_ref[gi], preferred_element_type=F32))
    y = (jnp.concatenate(ys, axis=1) * ps_ref[...]).astype(BF16)
    mix = (jnp.dot(y, wo_ref[:pwidth], preferred_element_type=F32)
           + jnp.dot(a_ref[0], wo_ref[pwidth:], preferred_element_type=F32))
    o_ref[0] = x_ref[0] + m_ref[0, 2:3, :] * _rms(mix, g_ref[...])


def _mix_out(u, attn, x, mods, g, pool_w, pool_scale, w_out, tm):
    b, l, d = x.shape
    pwidth, aw = u.shape[-1], attn.shape[-1]
    assert max(POOL_WINDOWS) // 2 <= HALO and tm % HALO == 0
    nh = tm // HALO
    last = l // HALO - 1
    tok = lambda w: pl.BlockSpec((1, tm, w), lambda i, j: (i, j, 0))
    return pl.pallas_call(
        functools.partial(_mix_out_kernel, seq=l),
        grid=(b, l // tm),
        in_specs=[tok(pwidth),
                  pl.BlockSpec((1, HALO, pwidth), lambda i, j: (i, jnp.maximum(j * nh - 1, 0), 0)),
                  pl.BlockSpec((1, HALO, pwidth), lambda i, j: (i, jnp.minimum((j + 1) * nh, last), 0)),
                  tok(aw), tok(d),
                  pl.BlockSpec((1, N_MOD, d), lambda i, j: (i, 0, 0)),
                  _resident((1, d)),
                  _resident(pool_w.shape),
                  _resident((1, pwidth)),
                  _resident(w_out.shape)],
        out_specs=tok(d),
        out_shape=jax.ShapeDtypeStruct((b, l, d), F32),
        compiler_params=_cparams(2),
        name="mix_out",
    )(u, u, u, attn, x, mods, g.reshape(1, d), pool_w, pool_scale.reshape(1, pwidth), w_out)


def _mlp_kernel(x_ref, m_ref, g1_ref, g2_ref, w1_ref, w2_ref, o_ref, acc_ref, *, tf):
    x = x_ref[0]
    h = (_rms(x, g1_ref[...]) * (1.0 + m_ref[0, 4:5, :]) + m_ref[0, 3:4, :]).astype(BF16)
    for c in range(w1_ref.shape[1] // tf):
        a = jnp.maximum(jnp.dot(h, w1_ref[:, c * tf:(c + 1) * tf], preferred_element_type=F32), 0.0)
        part = jnp.dot((a * a).astype(BF16), w2_ref[c * tf:(c + 1) * tf, :], preferred_element_type=F32)
        if c == 0:
            acc_ref[...] = part
        else:
            acc_ref[...] += part
    o_ref[0] = x + m_ref[0, 5:6, :] * _rms(acc_ref[...], g2_ref[...])


def _mlp(x, mods, g1, g2, w1, w2, tm, tf=1024):
    b, l, d = x.shape
    tok = pl.BlockSpec((1, tm, d), lambda i, j: (i, j, 0))
    return pl.pallas_call(
        functools.partial(_mlp_kernel, tf=tf),
        grid=(b, l // tm),
        in_specs=[tok,
                  pl.BlockSpec((1, N_MOD, d), lambda i, j: (i, 0, 0)),
                  _resident((1, d)), _resident((1, d)),
                  _resident(w1.shape), _resident(w2.shape)],
        out_specs=tok,
        out_shape=jax.ShapeDtypeStruct((b, l, d), F32),
        scratch_shapes=[pltpu.VMEM((tm, d), F32)],
        compiler_params=_cparams(2),
        name="mlp",
    )(x, mods, g1.reshape(1, d), g2.reshape(1, d), w1, w2)


def kernel(x, c, ctx, c_ctx, w_mod, b_mod, g_pre_mix, g_post_mix, g_pre_mlp, g_post_mlp,
           w_in, pool_w, pool_scale, rpb, w_out, w_mlp_in, w_mlp_out):
    b, l, d = x.shape
    depth = w_mod.shape[0]
    n_ctx = ctx.shape[1]
    pool_width = pool_scale.shape[-1]
    rows = l // GRID_W
    tm_x = min(1024, l)

    mod_rows = -(-(b + 1) // 8) * 8
    cc = jnp.zeros((mod_rows, d), F32).at[:b].set(c).at[b].set(c_ctx)
    mods = _modulation(cc, w_mod, b_mod).reshape(depth, mod_rows, N_MOD, d)

    w_in_b, pool_w_b, w_out_b = w_in.astype(BF16), pool_w.astype(BF16), w_out.astype(BF16)
    w1_b, w2_b = w_mlp_in.astype(BF16), w_mlp_out.astype(BF16)
    aw = (w_in.shape[-1] - pool_width) // 3
    w_kt_b = jnp.swapaxes(w_in_b[:, :, pool_width + aw:pool_width + 2 * aw], 1, 2)

    info, types, kh, kr = _na_structure(rows)
    info = jnp.asarray(info)

    for li in range(depth):
        mx = mods[li, :b]
        mc = jnp.broadcast_to(mods[li, b], (b, N_MOD, d))
        u, q, k, v = _in_projection(x, mx, g_pre_mix[li], w_in_b[li], w_kt_b[li], pool_width, tm_x)
        uc, qc, kc, vc = _in_projection(ctx, mc, g_pre_mix[li], w_in_b[li], w_kt_b[li], pool_width, n_ctx)
        table = _na_bias_table(rpb[li], types, kh, kr)
        attn = _neighbourhood_attention(q, k, v, kc, vc, table, info)
        x = _mix_out(u, attn, x, mx, g_post_mix[li], pool_w_b[li], pool_scale[li], w_out_b[li], tm_x)
        x = _mlp(x, mx, g_pre_mlp[li], g_post_mlp[li], w1_b[li], w2_b[li], tm_x)
        if li != depth - 1:
            attn_c = _context_attention(qc, kc, vc)
            ctx = _mix_out(uc, attn_c, ctx, mc, g_post_mix[li], pool_w_b[li], pool_scale[li], w_out_b[li], n_ctx)
            ctx = _mlp(ctx, mc, g_pre_mlp[li], g_post_mlp[li], w1_b[li], w2_b[li], n_ctx)
    return x
```

```python
import functools

import numpy as np
import jax
import jax.numpy as jnp
from jax import lax
from jax.experimental import pallas as pl
from jax.experimental.pallas import tpu as pltpu

F32 = jnp.float32
BF16 = jnp.bfloat16

EPS = 1e-6
GRID_W = 64
N_HEADS = 8
HEAD_DIM = 64
NA_KH = 8
NA_KW = 16
POOL_WINDOWS = (2, 4, 8, 16)
POOL_GROUP_DIM = 128
N_MOD = 6

LANES = 128
HALO = 8
NA_RB = 2
NA_ROWS_PER_STEP = 8
MIX_SUB = 256
VMEM_LIMIT = 56 * 1024 * 1024

_NT = (((1,), (1,)), ((), ()))


def _cparams(n_axes):
    return pltpu.CompilerParams(dimension_semantics=("parallel",) * n_axes,
                                vmem_limit_bytes=VMEM_LIMIT)


def _resident(shape):
    nd = len(shape)
    return pl.BlockSpec(shape, lambda *_: (0,) * nd, pipeline_mode=pl.Buffered(1))


def _rms(x, g):
    return x * lax.rsqrt(jnp.mean(x * x, axis=-1, keepdims=True) + EPS) * g


def _mod_kernel(cc_ref, w_ref, b_ref, o_ref):
    a = cc_ref[...]
    a = a / (1.0 + jnp.exp(-a))
    o_ref[0] = jnp.dot(a, w_ref[0], preferred_element_type=F32,
                       precision=lax.Precision.HIGHEST) + b_ref[0]


def _modulation(cc, w_mod, b_mod):
    depth, d, n = w_mod.shape
    rows = cc.shape[0]
    tn = 1536
    return pl.pallas_call(
        _mod_kernel,
        grid=(depth, n // tn),
        in_specs=[pl.BlockSpec((rows, d), lambda l, j: (0, 0)),
                  pl.BlockSpec((1, d, tn), lambda l, j: (l, 0, j)),
                  pl.BlockSpec((1, 1, tn), lambda l, j: (l, 0, j))],
        out_specs=pl.BlockSpec((1, rows, tn), lambda l, j: (l, 0, j)),
        out_shape=jax.ShapeDtypeStruct((depth, rows, n), F32),
        compiler_params=_cparams(2),
        name="adaln_mod",
    )(cc, w_mod, b_mod.reshape(depth, 1, n))


def _inproj_kernel(x_ref, m_ref, g_ref, w_ref, wkt_ref, u_ref, q_ref, kt_ref, v_ref):
    pw = u_ref.shape[-1]
    aw = q_ref.shape[-1]
    h = _rms(x_ref[0], g_ref[...]) * (1.0 + m_ref[0, 1:2, :]) + m_ref[0, 0:1, :]
    h = h.astype(BF16)
    u_ref[0] = jnp.dot(h, w_ref[:, :pw], preferred_element_type=F32)
    q = jnp.dot(h, w_ref[:, pw:pw + aw], preferred_element_type=F32)
    q_ref[0] = (q * (HEAD_DIM ** -0.5)).astype(BF16)
    kt_ref[0] = lax.dot_general(wkt_ref[...], h, _NT, preferred_element_type=F32).astype(BF16)
    v_ref[0] = jnp.dot(h, w_ref[:, pw + 2 * aw:], preferred_element_type=F32).astype(BF16)


def _in_projection(x, mods, g, w_in, w_kt, pool_width, tm):
    b, l, d = x.shape
    aw = w_kt.shape[0]
    tok = lambda w: pl.BlockSpec((1, tm, w), lambda i, j: (i, j, 0))
    return pl.pallas_call(
        _inproj_kernel,
        grid=(b, l // tm),
        in_specs=[tok(d),
                  pl.BlockSpec((1, N_MOD, d), lambda i, j: (i, 0, 0)),
                  _resident((1, d)),
                  _resident(w_in.shape),
                  _resident(w_kt.shape)],
        out_specs=[tok(pool_width), tok(aw), pl.BlockSpec((1, aw, tm), lambda i, j: (i, 0, j)), tok(aw)],
        out_shape=[jax.ShapeDtypeStruct((b, l, pool_width), F32),
                   jax.ShapeDtypeStruct((b, l, aw), BF16),
                   jax.ShapeDtypeStruct((b, aw, l), BF16),
                   jax.ShapeDtypeStruct((b, l, aw), BF16)],
        compiler_params=_cparams(2),
        name="in_proj",
    )(x, mods, g.reshape(1, d), w_in, w_kt)


def _na_structure(rows):
    kh = min(NA_KH, rows)
    step = LANES // GRID_W
    kr = -(-(NA_RB + kh - 1 + step - 1) // step) * step
    assert rows % NA_RB == 0 and rows % step == 0 and rows >= kr and GRID_W >= NA_KW
    types, info = {}, np.zeros((2, rows // NA_RB), np.int32)
    for p in range(rows // NA_RB):
        r = p * NA_RB + np.arange(NA_RB)
        rs = np.clip(r - kh // 2, 0, rows - kh)
        rs0 = int(np.clip(rs.min() // step * step, 0, rows - kr))
        assert rs0 % step == 0 and np.all(rs >= rs0) and np.all(rs + kh <= rs0 + kr)
        key = (tuple(int(a) for a in rs - rs0), tuple(int(a) for a in rs - r))
        info[0, p] = types.setdefault(key, len(types))
        info[1, p] = rs0
    return info, types, kh, kr


def _bias_table_kernel(rpb_ref, o_ref, *, types, kh, kr):
    n_dr, n_dc = 2 * NA_KH - 1, 2 * NA_KW - 1
    cq = lax.broadcasted_iota(jnp.int32, (GRID_W, GRID_W), 0)
    kc = lax.broadcasted_iota(jnp.int32, (GRID_W, GRID_W), 1)
    q_start = jnp.clip(cq - NA_KW // 2, 0, GRID_W - NA_KW)
    col_ok = (kc >= q_start) & (kc < q_start + NA_KW)
    dc = kc - cq + (NA_KW - 1)
    neg = jnp.full((GRID_W, GRID_W), -jnp.inf, F32)
    for s in range(2):
        h = pl.program_id(0) * 2 + s
        tiles = []
        for dr in range(n_dr):
            e = neg
            for j in range(n_dc):
                e = jnp.where(dc == j, rpb_ref[(h * n_dr + dr) * n_dc + j], e)
            tiles.append(jnp.where(col_ok, e, neg))
        for (off, rel), t in types.items():
            for tq in range(NA_RB):
                blocks = [tiles[rel[tq] + (j - off[tq]) + NA_KH - 1] if off[tq] <= j < off[tq] + kh else neg
                          for j in range(kr)]
                row0 = (s * NA_RB + tq) * GRID_W
                o_ref[0, t, row0:row0 + GRID_W, :] = jnp.concatenate(blocks, axis=1)


def _na_bias_table(rpb_l, types, kh, kr):
    h = rpb_l.shape[0]
    shape = (h // 2, len(types), 2 * NA_RB * GRID_W, kr * GRID_W)
    return pl.pallas_call(
        functools.partial(_bias_table_kernel, types=types, kh=kh, kr=kr),
        grid=(h // 2,),
        in_specs=[pl.BlockSpec(memory_space=pltpu.SMEM)],
        out_specs=pl.BlockSpec((1,) + shape[1:], lambda i: (i, 0, 0, 0)),
        out_shape=jax.ShapeDtypeStruct(shape, F32),
        compiler_params=_cparams(1),
        name="na_bias_table",
    )(rpb_l.reshape(-1))


def _pair_masks(n):
    lane = lax.broadcasted_iota(jnp.int32, (n, LANES), 1)
    first = lane < HEAD_DIM
    return first, first.astype(F32).astype(BF16), (~first).astype(F32).astype(BF16)


def _softmax_pv(parts):
    m = functools.reduce(jnp.maximum, [jnp.max(s, axis=1, keepdims=True) for s, _ in parts])
    es = [jnp.exp(s - m) for s, _ in parts]
    den = functools.reduce(jnp.add, [jnp.sum(e, axis=1, keepdims=True) for e in es])
    o = functools.reduce(jnp.add, [jnp.dot(e.astype(BF16), v, preferred_element_type=F32)
                                   for e, (_, v) in zip(es, parts)])
    return o / den


def _natten_kernel(info_ref, q_ref, kt_ref, v_ref, kct_ref, vc_ref, tab_ref, o_ref):
    nq = NA_RB * GRID_W
    nk = tab_ref.shape[-1]
    iters = q_ref.shape[1] // nq
    first, mask_a, mask_b = _pair_masks(nq)
    g = pl.program_id(1)

    def body(j, carry):
        p = g * iters + j
        typ = info_ref[0, p]
        qoff = pl.multiple_of(j * nq, nq)
        koff = pl.multiple_of(info_ref[1, p] * GRID_W, LANES)
        def scores(hp):
            ls = slice(hp * LANES, (hp + 1) * LANES)
            q2 = q_ref[0, pl.ds(qoff, nq), ls]
            qq = jnp.concatenate([q2 * mask_a, q2 * mask_b], axis=0)
            s_loc = jnp.dot(qq, kt_ref[0, ls, pl.ds(koff, nk)],
                            preferred_element_type=F32) + tab_ref[hp, typ]
            return s_loc, jnp.dot(qq, kct_ref[0, ls, :], preferred_element_type=F32)

        n_pairs = N_HEADS // 2
        nxt = scores(0)
        for hp in range(n_pairs):
            ls = slice(hp * LANES, (hp + 1) * LANES)
            (s_loc, s_ctx), nxt = nxt, (scores(hp + 1) if hp + 1 < n_pairs else None)
            o = _softmax_pv([(s_loc, v_ref[0, pl.ds(koff, nk), ls]), (s_ctx, vc_ref[0, :, ls])])
            o_ref[0, pl.ds(qoff, nq), ls] = jnp.where(first, o[:nq], o[nq:]).astype(BF16)
        return carry

    lax.fori_loop(0, iters, body, 0)


def _neighbourhood_attention(q, kt, v, kct, vc, table, info):
    b, l, aw = q.shape
    c = vc.shape[1]
    tq = NA_ROWS_PER_STEP * GRID_W
    per_batch = lambda shape: pl.BlockSpec((1,) + shape, lambda i, j, s: (i, 0, 0))
    grid_spec = pltpu.PrefetchScalarGridSpec(
        num_scalar_prefetch=1,
        grid=(b, l // tq),
        in_specs=[pl.BlockSpec((1, tq, aw), lambda i, j, s: (i, j, 0)),
                  per_batch((aw, l)), per_batch((l, aw)), per_batch((aw, c)), per_batch((c, aw)),
                  pl.BlockSpec(table.shape, lambda i, j, s: (0, 0, 0, 0), pipeline_mode=pl.Buffered(1))],
        out_specs=pl.BlockSpec((1, tq, aw), lambda i, j, s: (i, j, 0)))
    return pl.pallas_call(
        _natten_kernel,
        grid_spec=grid_spec,
        out_shape=jax.ShapeDtypeStruct((b, l, aw), BF16),
        compiler_params=_cparams(2),
        name="natten",
    )(info, q, kt, v, kct, vc, table)


def _ctx_attn_kernel(q_ref, kt_ref, v_ref, o_ref):
    n = q_ref.shape[1]
    first, mask_a, mask_b = _pair_masks(n)
    for hp in range(N_HEADS // 2):
        ls = slice(hp * LANES, (hp + 1) * LANES)
        q2 = q_ref[0, :, ls]
        qq = jnp.concatenate([q2 * mask_a, q2 * mask_b], axis=0)
        s = jnp.dot(qq, kt_ref[0, ls, :], preferred_element_type=F32)
        o = _softmax_pv([(s, v_ref[0, :, ls])])
        o_ref[0, :, ls] = jnp.where(first, o[:n], o[n:]).astype(BF16)


def _context_attention(q, kt, v):
    b, c, aw = q.shape
    spec = pl.BlockSpec((1, c, aw), lambda i: (i, 0, 0))
    return pl.pallas_call(
        _ctx_attn_kernel,
        grid=(b,),
        in_specs=[spec, pl.BlockSpec((1, aw, c), lambda i: (i, 0, 0)), spec],
        out_specs=spec,
        out_shape=jax.ShapeDtypeStruct((b, c, aw), BF16),
        compiler_params=_cparams(1),
        name="ctx_attn",
    )(q, kt, v)


def _mix_out_kernel(u_ref, up_ref, un_ref, a_ref, x_ref, m_ref, g_ref, pw_ref, ps_ref, wo_ref, o_ref, *, seq):
    tm, pwidth = u_ref.shape[1], u_ref.shape[2]
    sub = min(tm, MIX_SUB)
    i = pl.program_id(1)
    prev = jnp.where(i > 0, up_ref[0], 0.0)
    nxt = jnp.where(i < pl.num_programs(1) - 1, un_ref[0], 0.0)
    gain = g_ref[...] * m_ref[0, 2:3, :]

    def pooled(r0):
        before = prev if r0 == 0 else u_ref[0, r0 - HALO:r0, :]
        after = nxt if r0 + sub == tm else u_ref[0, r0 + sub:r0 + sub + HALO, :]
        ext = jnp.concatenate([before, u_ref[0, r0:r0 + sub, :], after], axis=0)
        edge_t = lax.broadcasted_iota(jnp.int32, (HALO, 1), 0) + (i * tm + r0)
        ys = []
        for gi, w in enumerate(POOL_WINDOWS):
            lo = w // 2
            hi = w - 1 - lo
            part = ext[:, gi * POOL_GROUP_DIM:(gi + 1) * POOL_GROUP_DIM]
            centre = part[HALO:HALO + sub]
            span = 1
            while 2 * span < w:
                n = part.shape[0] - span
                part = part[:n] + part[span:span + n]
                span *= 2
            tot = part[HALO - lo:HALO - lo + sub] + part[HALO - lo + span:HALO - lo + span + sub]
            d = tot * (1.0 / w) - centre
            fixes = []
            for e0 in (0, sub - HALO):
                tt = edge_t + e0
                cnt = (jnp.minimum(tt + hi, seq - 1) - jnp.maximum(tt - lo, 0) + 1).astype(F32)
                fixes.append(tot[e0:e0 + HALO] / cnt - centre[e0:e0 + HALO])
            d = jnp.concatenate([fixes[0], d[HALO:sub - HALO], fixes[1]], axis=0)
            ys.append(jnp.dot(d.astype(BF16), pw_ref[gi], preferred_element_type=F32))
        return (jnp.concatenate(ys, axis=1) * ps_ref[...]).astype(BF16)

    y = pooled(0)
    for r0 in range(0, tm, sub):
        rows = slice(r0, r0 + sub)
        mix = (jnp.dot(y, wo_ref[:pwidth], preferred_element_type=F32)
               + jnp.dot(a_ref[0, rows, :], wo_ref[pwidth:], preferred_element_type=F32))
        if r0 + sub < tm:
            y = pooled(r0 + sub)
        inv = lax.rsqrt(jnp.mean(mix * mix, axis=-1, keepdims=True) + EPS)
        o_ref[0, rows, :] = x_ref[0, rows, :] + (mix * inv) * gain


def _mix_out(u, attn, x, mods, g, pool_w, pool_scale, w_out, tm):
    b, l, d = x.shape
    pwidth, aw = u.shape[-1], attn.shape[-1]
    assert max(POOL_WINDOWS) // 2 <= HALO and tm % HALO == 0
    nh = tm // HALO
    last = l // HALO - 1
    tok = lambda w: pl.BlockSpec((1, tm, w), lambda i, j: (i, j, 0))
    return pl.pallas_call(
        functools.partial(_mix_out_kernel, seq=l),
        grid=(b, l // tm),
        in_specs=[tok(pwidth),
                  pl.BlockSpec((1, HALO, pwidth), lambda i, j: (i, jnp.maximum(j * nh - 1, 0), 0)),
                  pl.BlockSpec((1, HALO, pwidth), lambda i, j: (i, jnp.minimum((j + 1) * nh, last), 0)),
                  tok(aw), tok(d),
                  pl.BlockSpec((1, N_MOD, d), lambda i, j: (i, 0, 0)),
                  _resident((1, d)),
                  _resident(pool_w.shape),
                  _resident((1, pwidth)),
                  _resident(w_out.shape)],
        out_specs=tok(d),
        out_shape=jax.ShapeDtypeStruct((b, l, d), F32),
        compiler_params=_cparams(2),
        name="mix_out",
    )(u, u, u, attn, x, mods, g.reshape(1, d), pool_w, pool_scale.reshape(1, pwidth), w_out)


def _mlp_kernel(x_ref, m_ref, g1_ref, g2_ref, w1_ref, w2_ref, o_ref, acc_ref, *, tf):
    x = x_ref[0]
    h = (_rms(x, g1_ref[...]) * (1.0 + m_ref[0, 4:5, :]) + m_ref[0, 3:4, :]).astype(BF16)
    for c in range(w1_ref.shape[1] // tf):
        a = jnp.maximum(jnp.dot(h, w1_ref[:, c * tf:(c + 1) * tf], preferred_element_type=F32), 0.0)
        part = jnp.dot((a * a).astype(BF16), w2_ref[c * tf:(c + 1) * tf, :], preferred_element_type=F32)
        if c == 0:
            acc_ref[...] = part
        else:
            acc_ref[...] += part
    o_ref[0] = x + m_ref[0, 5:6, :] * _rms(acc_ref[...], g2_ref[...])


def _mlp(x, mods, g1, g2, w1, w2, tm, tf=1024):
    b, l, d = x.shape
    tok = pl.BlockSpec((1, tm, d), lambda i, j: (i, j, 0))
    return pl.pallas_call(
        functools.partial(_mlp_kernel, tf=tf),
        grid=(b, l // tm),
        in_specs=[tok,
                  pl.BlockSpec((1, N_MOD, d), lambda i, j: (i, 0, 0)),
                  _resident((1, d)), _resident((1, d)),
                  _resident(w1.shape), _resident(w2.shape)],
        out_specs=tok,
        out_shape=jax.ShapeDtypeStruct((b, l, d), F32),
        scratch_shapes=[pltpu.VMEM((tm, d), F32)],
        compiler_params=_cparams(2),
        name="mlp",
    )(x, mods, g1.reshape(1, d), g2.reshape(1, d), w1, w2)


def kernel(x, c, ctx, c_ctx, w_mod, b_mod, g_pre_mix, g_post_mix, g_pre_mlp, g_post_mlp,
           w_in, pool_w, pool_scale, rpb, w_out, w_mlp_in, w_mlp_out):
    b, l, d = x.shape
    depth = w_mod.shape[0]
    n_ctx = ctx.shape[1]
    pool_width = pool_scale.shape[-1]
    rows = l // GRID_W
    tm_x = min(1024, l)

    mod_rows = -(-(b + 1) // 8) * 8
    cc = jnp.zeros((mod_rows, d), F32).at[:b].set(c).at[b].set(c_ctx)
    mods = _modulation(cc, w_mod, b_mod).reshape(depth, mod_rows, N_MOD, d)

    w_in_b, pool_w_b, w_out_b = w_in.astype(BF16), pool_w.astype(BF16), w_out.astype(BF16)
    w1_b, w2_b = w_mlp_in.astype(BF16), w_mlp_out.astype(BF16)
    aw = (w_in.shape[-1] - pool_width) // 3
    w_kt_b = jnp.swapaxes(w_in_b[:, :, pool_width + aw:pool_width + 2 * aw], 1, 2)

    info, types, kh, kr = _na_structure(rows)
    info = jnp.asarray(info)

    for li in range(depth):
        mx = mods[li, :b]
        mc = jnp.broadcast_to(mods[li, b], (b, N_MOD, d))
        u, q, k, v = _in_projection(x, mx, g_pre_mix[li], w_in_b[li], w_kt_b[li], pool_width, tm_x)
        uc, qc, kc, vc = _in_projection(ctx, mc, g_pre_mix[li], w_in_b[li], w_kt_b[li], pool_width, n_ctx)
        table = _na_bias_table(rpb[li], types, kh, kr)
        attn = _neighbourhood_attention(q, k, v, kc, vc, table, info)
        x = _mix_out(u, attn, x, mx, g_post_mix[li], pool_w_b[li], pool_scale[li], w_out_b[li], tm_x)
        x = _mlp(x, mx, g_pre_mlp[li], g_post_mlp[li], w1_b[li], w2_b[li], tm_x)
        if li != depth - 1:
            attn_c = _context_attention(qc, kc, vc)
            ctx = _mix_out(uc, attn_c, ctx, mc, g_post_mix[li], pool_w_b[li], pool_scale[li], w_out_b[li], n_ctx)
            ctx = _mlp(ctx, mc, g_pre_mlp[li], g_post_mlp[li], w1_b[li], w2_b[li], n_ctx)
    return x
```

```python
import functools

import numpy as np
import jax
import jax.numpy as jnp
from jax import lax
from jax.experimental import pallas as pl
from jax.experimental.pallas import tpu as pltpu

F32 = jnp.float32
BF16 = jnp.bfloat16

EPS = 1e-6
GRID_W = 64
N_HEADS = 8
HEAD_DIM = 64
NA_KH = 8
NA_KW = 16
POOL_WINDOWS = (2, 4, 8, 16)
POOL_GROUP_DIM = 128
N_MOD = 6

LANES = 128
BF16_ROWS = 16
HALO = 8
NA_RB = 2
NA_ROWS_PER_STEP = 8
MIX_SUB = 256
PV_ROWS = LANES + BF16_ROWS
VMEM_LIMIT = 56 * 1024 * 1024

_NT = (((1,), (1,)), ((), ()))
N_PAIRS = N_HEADS // 2


def _cparams(n_axes):
    return pltpu.CompilerParams(dimension_semantics=("parallel",) * n_axes,
                                vmem_limit_bytes=VMEM_LIMIT)


def _layer(arr, li):
    nd = arr.ndim - 1
    return pl.BlockSpec((1,) + arr.shape[1:], lambda *_: (li,) + (0,) * nd, pipeline_mode=pl.Buffered(1))


def _mods_spec(mods, li, row_of):
    return pl.BlockSpec((1, 1) + mods.shape[2:], lambda i, j: (li, row_of(i), 0, 0))


def _rms(x, g):
    return x * lax.rsqrt(jnp.mean(x * x, axis=-1, keepdims=True) + EPS) * g


def _mod_kernel(cc_ref, w_ref, b_ref, o_ref):
    a = cc_ref[...]
    a = a / (1.0 + jnp.exp(-a))
    o_ref[0] = jnp.dot(a, w_ref[0], preferred_element_type=F32,
                       precision=lax.Precision.HIGHEST) + b_ref[0]


def _modulation(cc, w_mod, b_mod):
    depth, d, n = w_mod.shape
    rows = cc.shape[0]
    tn = 1536
    return pl.pallas_call(
        _mod_kernel,
        grid=(depth, n // tn),
        in_specs=[pl.BlockSpec((rows, d), lambda l, j: (0, 0)),
                  pl.BlockSpec((1, d, tn), lambda l, j: (l, 0, j)),
                  pl.BlockSpec((1, 1, tn), lambda l, j: (l, 0, j))],
        out_specs=pl.BlockSpec((1, rows, tn), lambda l, j: (l, 0, j)),
        out_shape=jax.ShapeDtypeStruct((depth, rows, n), F32),
        compiler_params=_cparams(2),
        name="adaln_mod",
    )(cc, w_mod, b_mod.reshape(depth, 1, n))


def _inproj_kernel(x_ref, m_ref, g_ref, wuk_ref, wqv_ref, u_ref, qt_ref, k_ref, vt_ref):
    pw = u_ref.shape[-1]
    aw = k_ref.shape[-1]
    tm = x_ref.shape[1]
    h = _rms(x_ref[0], g_ref[0]) * (1.0 + m_ref[0, 0, 1:2, :]) + m_ref[0, 0, 0:1, :]
    h = h.astype(BF16)
    u_ref[0] = jnp.dot(h, wuk_ref[0, :, :pw], preferred_element_type=F32)
    k_ref[0] = jnp.dot(h, wuk_ref[0, :, pw:], preferred_element_type=F32).astype(BF16)
    qv = lax.dot_general(wqv_ref[0], h, _NT, preferred_element_type=F32)
    qt_ref[0] = (qv[:aw] * (HEAD_DIM ** -0.5)).astype(BF16)
    ones = jnp.ones((BF16_ROWS, tm), BF16)
    for hp in range(N_PAIRS):
        vt_ref[0, hp * PV_ROWS:hp * PV_ROWS + LANES, :] = qv[aw + hp * LANES:aw + (hp + 1) * LANES].astype(BF16)
        vt_ref[0, hp * PV_ROWS + LANES:(hp + 1) * PV_ROWS, :] = ones


def _in_projection(x, mods, li, row_of, g, w_uk, w_qv, pool_width, tm):
    b, l, d = x.shape
    aw = w_uk.shape[-1] - pool_width
    assert aw == N_PAIRS * LANES
    tok = lambda w: pl.BlockSpec((1, tm, w), lambda i, j: (i, j, 0))
    feat = lambda r: pl.BlockSpec((1, r, tm), lambda i, j: (i, 0, j))
    return pl.pallas_call(
        _inproj_kernel,
        grid=(b, l // tm),
        in_specs=[tok(d), _mods_spec(mods, li, row_of), _layer(g, li), _layer(w_uk, li), _layer(w_qv, li)],
        out_specs=[tok(pool_width), feat(aw), tok(aw), feat(N_PAIRS * PV_ROWS)],
        out_shape=[jax.ShapeDtypeStruct((b, l, pool_width), F32),
                   jax.ShapeDtypeStruct((b, aw, l), BF16),
                   jax.ShapeDtypeStruct((b, l, aw), BF16),
                   jax.ShapeDtypeStruct((b, N_PAIRS * PV_ROWS, l), BF16)],
        compiler_params=_cparams(2),
        name="in_proj",
    )(x, mods, g, w_uk, w_qv)


def _na_structure(rows):
    kh = min(NA_KH, rows)
    step = LANES // GRID_W
    kr = -(-(NA_RB + kh - 1 + step - 1) // step) * step
    assert rows % NA_RB == 0 and rows % step == 0 and rows >= kr and GRID_W >= NA_KW
    types, info = {}, np.zeros((2, rows // NA_RB), np.int32)
    for p in range(rows // NA_RB):
        r = p * NA_RB + np.arange(NA_RB)
        rs = np.clip(r - kh // 2, 0, rows - kh)
        rs0 = int(np.clip(rs.min() // step * step, 0, rows - kr))
        assert rs0 % step == 0 and np.all(rs >= rs0) and np.all(rs + kh <= rs0 + kr)
        key = (tuple(int(a) for a in rs - rs0), tuple(int(a) for a in rs - r))
        info[0, p] = types.setdefault(key, len(types))
        info[1, p] = rs0
    return info, types, kh, kr


def _bias_table_kernel(rpb_ref, o_ref, *, types, kh, kr):
    n_dr, n_dc = 2 * NA_KH - 1, 2 * NA_KW - 1
    kc = lax.broadcasted_iota(jnp.int32, (GRID_W, GRID_W), 0)
    cq = lax.broadcasted_iota(jnp.int32, (GRID_W, GRID_W), 1)
    q_start = jnp.clip(cq - NA_KW // 2, 0, GRID_W - NA_KW)
    col_ok = (kc >= q_start) & (kc < q_start + NA_KW)
    dc = kc - cq + (NA_KW - 1)
    neg = jnp.full((GRID_W, GRID_W), -jnp.inf, F32)
    for s in range(2):
        h = pl.program_id(0) * 2 + s
        tiles = []
        for dr in range(n_dr):
            e = neg
            for j in range(n_dc):
                e = jnp.where(dc == j, rpb_ref[(h * n_dr + dr) * n_dc + j], e)
            tiles.append(jnp.where(col_ok, e, neg))
        for (off, rel), t in types.items():
            for tq in range(NA_RB):
                blocks = [tiles[rel[tq] + (j - off[tq]) + NA_KH - 1] if off[tq] <= j < off[tq] + kh else neg
                          for j in range(kr)]
                col0 = (s * NA_RB + tq) * GRID_W
                o_ref[0, t, :, col0:col0 + GRID_W] = jnp.concatenate(blocks, axis=0)


def _na_bias_table(rpb, li, types, kh, kr):
    depth, h = rpb.shape[:2]
    shape = (h // 2, len(types), kr * GRID_W, 2 * NA_RB * GRID_W)
    return pl.pallas_call(
        functools.partial(_bias_table_kernel, types=types, kh=kh, kr=kr),
        grid=(h // 2,),
        in_specs=[pl.BlockSpec(memory_space=pltpu.SMEM)],
        out_specs=pl.BlockSpec((1,) + shape[1:], lambda i: (i, 0, 0, 0)),
        out_shape=jax.ShapeDtypeStruct(shape, F32),
        compiler_params=_cparams(1),
        name="na_bias_table",
    )(rpb[li].reshape(-1))


def _head_row_masks(n):
    first = lax.broadcasted_iota(jnp.int32, (LANES, n), 0) < HEAD_DIM
    return first.astype(F32).astype(BF16), (~first).astype(F32).astype(BF16)


def _pair_scores(qt2, masks, keys):
    qq = jnp.concatenate([qt2 * masks[0], qt2 * masks[1]], axis=1)
    out = []
    for k2, bias in keys:
        s = jnp.dot(k2, qq, preferred_element_type=F32)
        out.append(s if bias is None else s + bias)
    return out


def _pair_output(scores, values):
    n = scores[0].shape[1] // 2
    m = functools.reduce(jnp.maximum, [jnp.max(s, axis=0, keepdims=True) for s in scores])
    o = functools.reduce(jnp.add, [jnp.dot(vt, jnp.exp(s - m).astype(BF16), preferred_element_type=F32)
                                   for s, vt in zip(scores, values)])
    o = o[:LANES] * (1.0 / o[LANES:LANES + 1])
    return jnp.concatenate([o[:HEAD_DIM, :n], o[HEAD_DIM:, n:]], axis=0).T


def _natten_kernel(info_ref, qt_ref, k_ref, vt_ref, kc_ref, vct_ref, tab_ref, o_ref):
    nq = NA_RB * GRID_W
    nk = tab_ref.shape[-2]
    iters = qt_ref.shape[2] // nq
    masks = _head_row_masks(nq)
    g = pl.program_id(1)

    koffs = [pl.multiple_of(info_ref[1, g * iters + j] * GRID_W, LANES) for j in range(iters)]
    units = [(j, hp) for j in range(iters) for hp in range(N_PAIRS)]

    def scores(j, hp):
        fs = slice(hp * LANES, (hp + 1) * LANES)
        bias = tab_ref[hp, info_ref[0, g * iters + j]]
        return _pair_scores(qt_ref[0, fs, j * nq:(j + 1) * nq], masks,
                            [(k_ref[0, pl.ds(koffs[j], nk), fs], bias), (kc_ref[0, :, fs], None)])

    nxt = scores(*units[0])
    for n, (j, hp) in enumerate(units):
        cur, nxt = nxt, (scores(*units[n + 1]) if n + 1 < len(units) else None)
        vs = slice(hp * PV_ROWS, (hp + 1) * PV_ROWS)
        out = _pair_output(cur, [vt_ref[0, vs, pl.ds(koffs[j], nk)], vct_ref[0, vs, :]])
        o_ref[0, j * nq:(j + 1) * nq, hp * LANES:(hp + 1) * LANES] = out.astype(BF16)


def _neighbourhood_attention(qt, k, vt, kc, vct, table, info):
    b, l, aw = k.shape
    c = kc.shape[1]
    assert NA_RB * GRID_W == LANES
    tq = NA_ROWS_PER_STEP * GRID_W
    per_batch = lambda arr: pl.BlockSpec((1,) + arr.shape[1:], lambda i, j, s: (i, 0, 0))
    grid_spec = pltpu.PrefetchScalarGridSpec(
        num_scalar_prefetch=1,
        grid=(b, l // tq),
        in_specs=[pl.BlockSpec((1, aw, tq), lambda i, j, s: (i, 0, j)),
                  per_batch(k), per_batch(vt), per_batch(kc), per_batch(vct),
                  pl.BlockSpec(table.shape, lambda i, j, s: (0, 0, 0, 0), pipeline_mode=pl.Buffered(1))],
        out_specs=pl.BlockSpec((1, tq, aw), lambda i, j, s: (i, j, 0)))
    return pl.pallas_call(
        _natten_kernel,
        grid_spec=grid_spec,
        out_shape=jax.ShapeDtypeStruct((b, l, aw), BF16),
        compiler_params=_cparams(2),
        name="natten",
    )(info, qt, k, vt, kc, vct, table)


def _ctx_attn_kernel(qt_ref, k_ref, vt_ref, o_ref):
    masks = _head_row_masks(qt_ref.shape[2])
    for hp in range(N_PAIRS):
        fs = slice(hp * LANES, (hp + 1) * LANES)
        s = _pair_scores(qt_ref[0, fs, :], masks, [(k_ref[0, :, fs], None)])
        o_ref[0, :, fs] = _pair_output(s, [vt_ref[0, hp * PV_ROWS:(hp + 1) * PV_ROWS, :]]).astype(BF16)


def _context_attention(qt, k, vt):
    b, c, aw = k.shape
    whole = lambda arr: pl.BlockSpec((1,) + arr.shape[1:], lambda i: (i, 0, 0))
    return pl.pallas_call(
        _ctx_attn_kernel,
        grid=(b,),
        in_specs=[whole(qt), whole(k), whole(vt)],
        out_specs=whole(k),
        out_shape=jax.ShapeDtypeStruct((b, c, aw), BF16),
        compiler_params=_cparams(1),
        name="ctx_attn",
    )(qt, k, vt)


def _tail_kernel(u_ref, up_ref, un_ref, a_ref, x_ref, m_ref, gmix_ref, gpre_ref, gpost_ref, pw_ref, ps_ref,
                 wo_ref, w1_ref, w2_ref, o_ref, mid_ref, acc_ref, *, seq, tf):
    tm, pwidth = u_ref.shape[1], u_ref.shape[2]
    sub = min(tm, MIX_SUB)
    i = pl.program_id(1)
    prev = jnp.where(i > 0, up_ref[0], 0.0)
    nxt = jnp.where(i < pl.num_programs(1) - 1, un_ref[0], 0.0)
    mod = lambda k: m_ref[0, 0, k:k + 1, :]
    gain = gmix_ref[0] * mod(2)

    def pooled(r0):
        before = prev if r0 == 0 else u_ref[0, r0 - HALO:r0, :]
        after = nxt if r0 + sub == tm else u_ref[0, r0 + sub:r0 + sub + HALO, :]
        ext = jnp.concatenate([before, u_ref[0, r0:r0 + sub, :], after], axis=0)
        edge_t = lax.broadcasted_iota(jnp.int32, (HALO, 1), 0) + (i * tm + r0)
        ys = []
        for gi, w in enumerate(POOL_WINDOWS):
            lo = w // 2
            hi = w - 1 - lo
            part = ext[:, gi * POOL_GROUP_DIM:(gi + 1) * POOL_GROUP_DIM]
            centre = part[HALO:HALO + sub]
            span = 1
            while 2 * span < w:
                n = part.shape[0] - span
                part = part[:n] + part[span:span + n]
                span *= 2
            tot = part[HALO - lo:HALO - lo + sub] + part[HALO - lo + span:HALO - lo + span + sub]
            d = tot * (1.0 / w) - centre
            fixes = []
            for e0 in (0, sub - HALO):
                tt = edge_t + e0
                cnt = (jnp.minimum(tt + hi, seq - 1) - jnp.maximum(tt - lo, 0) + 1).astype(F32)
                fixes.append(tot[e0:e0 + HALO] / cnt - centre[e0:e0 + HALO])
            d = jnp.concatenate([fixes[0], d[HALO:sub - HALO], fixes[1]], axis=0)
            ys.append(jnp.dot(d.astype(BF16), pw_ref[0, gi], preferred_element_type=F32))
        return (jnp.concatenate(ys, axis=1) * ps_ref[0]).astype(BF16)

    def mixed(r0, y):
        rows = slice(r0, r0 + sub)
        mix = (jnp.dot(y, wo_ref[0, :pwidth], preferred_element_type=F32)
               + jnp.dot(a_ref[0, rows, :], wo_ref[0, pwidth:], preferred_element_type=F32))
        inv = lax.rsqrt(jnp.mean(mix * mix, axis=-1, keepdims=True) + EPS)
        mid_ref[rows, :] = x_ref[0, rows, :] + (mix * inv) * gain
        return (_rms(mid_ref[rows, :], gpre_ref[0]) * (1.0 + mod(4)) + mod(3)).astype(BF16)

    def mlp_chunk(r0, h, c):
        rows = slice(r0, r0 + sub)
        a = jnp.maximum(jnp.dot(h, w1_ref[0, :, c * tf:(c + 1) * tf], preferred_element_type=F32), 0.0)
        part = jnp.dot((a * a).astype(BF16), w2_ref[0, c * tf:(c + 1) * tf, :], preferred_element_type=F32)
        if c == 0:
            acc_ref[rows, :] = part
        else:
            acc_ref[rows, :] += part

    n_chunks = w1_ref.shape[2] // tf
    h = mixed(0, pooled(0))
    for r0 in range(0, tm, sub):
        rows = slice(r0, r0 + sub)
        more = r0 + sub < tm
        for c in range(n_chunks):
            mlp_chunk(r0, h, c)
            if more and c == 0:
                y_next = pooled(r0 + sub)
            if more and c == max(n_chunks - 2, 0):
                h_next = mixed(r0 + sub, y_next)
        o_ref[0, rows, :] = mid_ref[rows, :] + mod(5) * _rms(acc_ref[rows, :], gpost_ref[0])
        if more:
            h = h_next


def _block_tail(u, attn, x, mods, li, row_of, g_mix, g_pre, g_post, pool_w, pool_scale, w_out, w1, w2, tm, tf=1024):
    b, l, d = x.shape
    pwidth, aw = u.shape[-1], attn.shape[-1]
    assert max(POOL_WINDOWS) // 2 <= HALO and tm % HALO == 0
    nh = tm // HALO
    last = l // HALO - 1
    tok = lambda w: pl.BlockSpec((1, tm, w), lambda i, j: (i, j, 0))
    return pl.pallas_call(
        functools.partial(_tail_kernel, seq=l, tf=tf),
        grid=(b, l // tm),
        in_specs=[tok(pwidth),
                  pl.BlockSpec((1, HALO, pwidth), lambda i, j: (i, jnp.maximum(j * nh - 1, 0), 0)),
                  pl.BlockSpec((1, HALO, pwidth), lambda i, j: (i, jnp.minimum((j + 1) * nh, last), 0)),
                  tok(aw), tok(d), _mods_spec(mods, li, row_of),
                  _layer(g_mix, li), _layer(g_pre, li), _layer(g_post, li),
                  _layer(pool_w, li), _layer(pool_scale, li), _layer(w_out, li), _layer(w1, li), _layer(w2, li)],
        out_specs=tok(d),
        out_shape=jax.ShapeDtypeStruct((b, l, d), F32),
        scratch_shapes=[pltpu.VMEM((tm, d), F32), pltpu.VMEM((tm, d), F32)],
        compiler_params=_cparams(2),
        name="block_tail",
    )(u, u, u, attn, x, mods, g_mix, g_pre, g_post, pool_w, pool_scale, w_out, w1, w2)


def kernel(x, c, ctx, c_ctx, w_mod, b_mod, g_pre_mix, g_post_mix, g_pre_mlp, g_post_mlp,
           w_in, pool_w, pool_scale, rpb, w_out, w_mlp_in, w_mlp_out):
    b, l, d = x.shape
    depth = w_mod.shape[0]
    n_ctx = ctx.shape[1]
    pool_width = pool_scale.shape[-1]
    aw = (w_in.shape[-1] - pool_width) // 3
    rows = l // GRID_W
    tm_in = min(1024, l)
    tm_tail = min(1024, l)

    mod_rows = -(-(b + 1) // 8) * 8
    cc = jnp.zeros((mod_rows, d), F32).at[:b].set(c).at[b].set(c_ctx)
    mods = _modulation(cc, w_mod, b_mod).reshape(depth, mod_rows, N_MOD, d)
    x_row, ctx_row = (lambda i: i), (lambda i: b)

    w_in_b = w_in.astype(BF16)
    w_uk = jnp.concatenate([w_in_b[..., :pool_width], w_in_b[..., pool_width + aw:pool_width + 2 * aw]], axis=-1)
    w_qv = jnp.swapaxes(jnp.concatenate([w_in_b[..., pool_width:pool_width + aw],
                                         w_in_b[..., pool_width + 2 * aw:]], axis=-1), 1, 2)
    pool_w_b, w_out_b = pool_w.astype(BF16), w_out.astype(BF16)
    w1_b, w2_b = w_mlp_in.astype(BF16), w_mlp_out.astype(BF16)
    vec = lambda a: a.reshape(depth, 1, a.shape[-1])
    g_pre_mix, g_post_mix, g_pre_mlp, g_post_mlp, pool_scale = map(
        vec, (g_pre_mix, g_post_mix, g_pre_mlp, g_post_mlp, pool_scale))

    info, types, kh, kr = _na_structure(rows)
    info = jnp.asarray(info)

    for li in range(depth):
        tail = functools.partial(_block_tail, mods=mods, li=li, g_mix=g_post_mix, g_pre=g_pre_mlp, g_post=g_post_mlp,
                                 pool_w=pool_w_b, pool_scale=pool_scale, w_out=w_out_b, w1=w1_b, w2=w2_b)
        u, qt, k, vt = _in_projection(x, mods, li, x_row, g_pre_mix, w_uk, w_qv, pool_width, tm_in)
        uc, qct, kc, vct = _in_projection(ctx, mods, li, ctx_row, g_pre_mix, w_uk, w_qv, pool_width, n_ctx)
        table = _na_bias_table(rpb, li, types, kh, kr)
        attn = _neighbourhood_attention(qt, k, vt, kc, vct, table, info)
        x = tail(u, attn, x, row_of=x_row, tm=tm_tail)
        if li != depth - 1:
            ctx = tail(uc, _context_attention(qct, kc, vct), ctx, row_of=ctx_row, tm=n_ctx)
    return x
```

```python
import functools

import numpy as np
import jax
import jax.numpy as jnp
from jax import lax
from jax.experimental import pallas as pl
from jax.experimental.pallas import tpu as pltpu

F32 = jnp.float32
BF16 = jnp.bfloat16

EPS = 1e-6
GRID_W = 64
N_HEADS = 8
HEAD_DIM = 64
NA_KH = 8
NA_KW = 16
POOL_WINDOWS = (2, 4, 8, 16)
POOL_GROUP_DIM = 128
N_MOD = 6

LANES = 128
BF16_ROWS = 16
HALO = 8
NA_RB = 2
NA_ROWS_PER_STEP = 16
MIX_SUB = 256
PV_ROWS = LANES + BF16_ROWS
VMEM_LIMIT = 56 * 1024 * 1024

LOG2E = 1.4426950408889634
Q_SCALE = HEAD_DIM ** -0.5 * LOG2E

_NT = (((1,), (1,)), ((), ()))
N_PAIRS = N_HEADS // 2


def _cparams(n_axes):
    return pltpu.CompilerParams(dimension_semantics=("parallel",) * n_axes,
                                vmem_limit_bytes=VMEM_LIMIT)


def _layer(arr, li):
    nd = arr.ndim - 1
    return pl.BlockSpec((1,) + arr.shape[1:], lambda *_: (li,) + (0,) * nd, pipeline_mode=pl.Buffered(1))


def _mods_spec(mods, li, row_of):
    return pl.BlockSpec((1, 1) + mods.shape[2:], lambda i, j: (li, row_of(i), 0, 0))


def _rms(x, g):
    return x * lax.rsqrt(jnp.mean(x * x, axis=-1, keepdims=True) + EPS) * g


def _mod_kernel(cc_ref, w_ref, b_ref, o_ref):
    a = cc_ref[...]
    a = a / (1.0 + jnp.exp(-a))
    o_ref[0] = jnp.dot(a.astype(BF16), w_ref[0].astype(BF16), preferred_element_type=F32) + b_ref[0]


def _modulation(cc, w_mod, b_mod):
    depth, d, n = w_mod.shape
    rows = cc.shape[0]
    tn = 1536
    return pl.pallas_call(
        _mod_kernel,
        grid=(depth, n // tn),
        in_specs=[pl.BlockSpec((rows, d), lambda l, j: (0, 0)),
                  pl.BlockSpec((1, d, tn), lambda l, j: (l, 0, j)),
                  pl.BlockSpec((1, 1, tn), lambda l, j: (l, 0, j))],
        out_specs=pl.BlockSpec((1, rows, tn), lambda l, j: (l, 0, j)),
        out_shape=jax.ShapeDtypeStruct((depth, rows, n), F32),
        compiler_params=_cparams(2),
        name="adaln_mod",
    )(cc, w_mod, b_mod.reshape(depth, 1, n))


def _inproj_kernel(x_ref, m_ref, g_ref, wuk_ref, wqv_ref, u_ref, qt_ref, k_ref, vt_ref):
    pw = u_ref.shape[-1]
    aw = k_ref.shape[-1]
    tm = x_ref.shape[1]
    h = _rms(x_ref[0], g_ref[0]) * (1.0 + m_ref[0, 0, 1:2, :]) + m_ref[0, 0, 0:1, :]
    h = h.astype(BF16)
    u_ref[0] = jnp.dot(h, wuk_ref[0, :, :pw], preferred_element_type=F32)
    k_ref[0] = jnp.dot(h, wuk_ref[0, :, pw:], preferred_element_type=F32).astype(BF16)
    qv = lax.dot_general(wqv_ref[0], h, _NT, preferred_element_type=F32)
    qt_ref[0] = (qv[:aw] * Q_SCALE).astype(BF16)
    ones = jnp.ones((BF16_ROWS, tm), BF16)
    for hp in range(N_PAIRS):
        vt_ref[0, hp * PV_ROWS:hp * PV_ROWS + LANES, :] = qv[aw + hp * LANES:aw + (hp + 1) * LANES].astype(BF16)
        vt_ref[0, hp * PV_ROWS + LANES:(hp + 1) * PV_ROWS, :] = ones


def _in_projection(x, mods, li, row_of, g, w_uk, w_qv, pool_width, tm):
    b, l, d = x.shape
    aw = w_uk.shape[-1] - pool_width
    assert aw == N_PAIRS * LANES
    tok = lambda w: pl.BlockSpec((1, tm, w), lambda i, j: (i, j, 0))
    feat = lambda r: pl.BlockSpec((1, r, tm), lambda i, j: (i, 0, j))
    return pl.pallas_call(
        _inproj_kernel,
        grid=(b, l // tm),
        in_specs=[tok(d), _mods_spec(mods, li, row_of), _layer(g, li), _layer(w_uk, li), _layer(w_qv, li)],
        out_specs=[tok(pool_width), feat(aw), tok(aw), feat(N_PAIRS * PV_ROWS)],
        out_shape=[jax.ShapeDtypeStruct((b, l, pool_width), F32),
                   jax.ShapeDtypeStruct((b, aw, l), BF16),
                   jax.ShapeDtypeStruct((b, l, aw), BF16),
                   jax.ShapeDtypeStruct((b, N_PAIRS * PV_ROWS, l), BF16)],
        compiler_params=_cparams(2),
        name="in_proj",
    )(x, mods, g, w_uk, w_qv)


def _na_structure(rows):
    kh = min(NA_KH, rows)
    step = LANES // GRID_W
    kr = -(-(NA_RB + kh - 1 + step - 1) // step) * step
    assert rows % NA_RB == 0 and rows % step == 0 and rows >= kr and GRID_W >= NA_KW
    types, info = {}, np.zeros((2, rows // NA_RB), np.int32)
    for p in range(rows // NA_RB):
        r = p * NA_RB + np.arange(NA_RB)
        rs = np.clip(r - kh // 2, 0, rows - kh)
        rs0 = int(np.clip(rs.min() // step * step, 0, rows - kr))
        assert rs0 % step == 0 and np.all(rs >= rs0) and np.all(rs + kh <= rs0 + kr)
        key = (tuple(int(a) for a in rs - rs0), tuple(int(a) for a in rs - r))
        info[0, p] = types.setdefault(key, len(types))
        info[1, p] = rs0
    return info, types, kh, kr


def _bias_table_kernel(rpb_ref, o_ref, *, types, kh, kr):
    n_dr, n_dc = 2 * NA_KH - 1, 2 * NA_KW - 1
    kc = lax.broadcasted_iota(jnp.int32, (GRID_W, GRID_W), 0)
    cq = lax.broadcasted_iota(jnp.int32, (GRID_W, GRID_W), 1)
    q_start = jnp.clip(cq - NA_KW // 2, 0, GRID_W - NA_KW)
    col_ok = (kc >= q_start) & (kc < q_start + NA_KW)
    dc = kc - cq + (NA_KW - 1)
    neg = jnp.full((GRID_W, GRID_W), -jnp.inf, F32)
    for s in range(2):
        h = pl.program_id(0) * 2 + s
        tiles = []
        for dr in range(n_dr):
            e = neg
            for j in range(n_dc):
                e = jnp.where(dc == j, rpb_ref[(h * n_dr + dr) * n_dc + j] * LOG2E, e)
            tiles.append(jnp.where(col_ok, e, neg))
        for (off, rel), t in types.items():
            for tq in range(NA_RB):
                blocks = [tiles[rel[tq] + (j - off[tq]) + NA_KH - 1] if off[tq] <= j < off[tq] + kh else neg
                          for j in range(kr)]
                col0 = (s * NA_RB + tq) * GRID_W
                o_ref[0, t, :, col0:col0 + GRID_W] = jnp.concatenate(blocks, axis=0)


def _na_bias_table(rpb, li, types, kh, kr):
    depth, h = rpb.shape[:2]
    shape = (h // 2, len(types), kr * GRID_W, 2 * NA_RB * GRID_W)
    return pl.pallas_call(
        functools.partial(_bias_table_kernel, types=types, kh=kh, kr=kr),
        grid=(h // 2,),
        in_specs=[pl.BlockSpec(memory_space=pltpu.SMEM)],
        out_specs=pl.BlockSpec((1,) + shape[1:], lambda i: (i, 0, 0, 0)),
        out_shape=jax.ShapeDtypeStruct(shape, F32),
        compiler_params=_cparams(1),
        name="na_bias_table",
    )(rpb[li].reshape(-1))


def _head_row_masks(n):
    first = lax.broadcasted_iota(jnp.int32, (LANES, n), 0) < HEAD_DIM
    return first.astype(F32).astype(BF16), (~first).astype(F32).astype(BF16)


def _pair_scores(qt2, masks, keys):
    qq = jnp.concatenate([qt2 * masks[0], qt2 * masks[1]], axis=1)
    out = []
    for k2, bias in keys:
        s = jnp.dot(k2, qq, preferred_element_type=F32)
        out.append(s if bias is None else s + bias)
    return out


def _pair_output(scores, values):
    n = scores[0].shape[1] // 2
    m = functools.reduce(jnp.maximum, [jnp.max(s, axis=0, keepdims=True) for s in scores])
    o = functools.reduce(jnp.add, [jnp.dot(vt, jnp.exp2(s - m).astype(BF16), preferred_element_type=F32)
                                   for s, vt in zip(scores, values)])
    o = o[:LANES] * (1.0 / o[LANES:LANES + 1])
    return jnp.concatenate([o[:HEAD_DIM, :n], o[HEAD_DIM:, n:]], axis=0).T


def _natten_kernel(info_ref, qt_ref, k_ref, vt_ref, kc_ref, vct_ref, tab_ref, o_ref):
    nq = NA_RB * GRID_W
    nk = tab_ref.shape[-2]
    iters = qt_ref.shape[2] // nq
    masks = _head_row_masks(nq)
    g = pl.program_id(1)

    koffs = [pl.multiple_of(info_ref[1, g * iters + j] * GRID_W, LANES) for j in range(iters)]
    units = [(j, hp) for j in range(iters) for hp in range(N_PAIRS)]

    def scores(j, hp):
        fs = slice(hp * LANES, (hp + 1) * LANES)
        bias = tab_ref[hp, info_ref[0, g * iters + j]]
        return _pair_scores(qt_ref[0, fs, j * nq:(j + 1) * nq], masks,
                            [(k_ref[0, pl.ds(koffs[j], nk), fs], bias), (kc_ref[0, :, fs], None)])

    nxt = scores(*units[0])
    for n, (j, hp) in enumerate(units):
        cur, nxt = nxt, (scores(*units[n + 1]) if n + 1 < len(units) else None)
        vs = slice(hp * PV_ROWS, (hp + 1) * PV_ROWS)
        out = _pair_output(cur, [vt_ref[0, vs, pl.ds(koffs[j], nk)], vct_ref[0, vs, :]])
        o_ref[0, j * nq:(j + 1) * nq, hp * LANES:(hp + 1) * LANES] = out.astype(BF16)


def _neighbourhood_attention(qt, k, vt, kc, vct, table, info):
    b, l, aw = k.shape
    c = kc.shape[1]
    assert NA_RB * GRID_W == LANES
    tq = NA_ROWS_PER_STEP * GRID_W
    per_batch = lambda arr: pl.BlockSpec((1,) + arr.shape[1:], lambda i, j, s: (i, 0, 0))
    grid_spec = pltpu.PrefetchScalarGridSpec(
        num_scalar_prefetch=1,
        grid=(b, l // tq),
        in_specs=[pl.BlockSpec((1, aw, tq), lambda i, j, s: (i, 0, j)),
                  per_batch(k), per_batch(vt), per_batch(kc), per_batch(vct),
                  pl.BlockSpec(table.shape, lambda i, j, s: (0, 0, 0, 0), pipeline_mode=pl.Buffered(1))],
        out_specs=pl.BlockSpec((1, tq, aw), lambda i, j, s: (i, j, 0)))
    return pl.pallas_call(
        _natten_kernel,
        grid_spec=grid_spec,
        out_shape=jax.ShapeDtypeStruct((b, l, aw), BF16),
        compiler_params=_cparams(2),
        name="natten",
    )(info, qt, k, vt, kc, vct, table)


def _ctx_attn_kernel(qt_ref, k_ref, vt_ref, o_ref):
    masks = _head_row_masks(qt_ref.shape[2])
    for hp in range(N_PAIRS):
        fs = slice(hp * LANES, (hp + 1) * LANES)
        s = _pair_scores(qt_ref[0, fs, :], masks, [(k_ref[0, :, fs], None)])
        o_ref[0, :, fs] = _pair_output(s, [vt_ref[0, hp * PV_ROWS:(hp + 1) * PV_ROWS, :]]).astype(BF16)


def _context_attention(qt, k, vt):
    b, c, aw = k.shape
    whole = lambda arr: pl.BlockSpec((1,) + arr.shape[1:], lambda i: (i, 0, 0))
    return pl.pallas_call(
        _ctx_attn_kernel,
        grid=(b,),
        in_specs=[whole(qt), whole(k), whole(vt)],
        out_specs=whole(k),
        out_shape=jax.ShapeDtypeStruct((b, c, aw), BF16),
        compiler_params=_cparams(1),
        name="ctx_attn",
    )(qt, k, vt)


def _tail_kernel(u_ref, up_ref, un_ref, a_ref, x_ref, m_ref, gmix_ref, gpre_ref, gpost_ref, pw_ref, ps_ref,
                 wo_ref, w1_ref, w2_ref, o_ref, mid_ref, acc_ref, *, seq, tf):
    tm, pwidth = u_ref.shape[1], u_ref.shape[2]
    sub = min(tm, MIX_SUB)
    i = pl.program_id(1)
    prev = jnp.where(i > 0, up_ref[0], 0.0)
    nxt = jnp.where(i < pl.num_programs(1) - 1, un_ref[0], 0.0)
    mod = lambda k: m_ref[0, 0, k:k + 1, :]
    gain = gmix_ref[0] * mod(2)

    def pooled(r0):
        before = prev if r0 == 0 else u_ref[0, r0 - HALO:r0, :]
        after = nxt if r0 + sub == tm else u_ref[0, r0 + sub:r0 + sub + HALO, :]
        ext = jnp.concatenate([before, u_ref[0, r0:r0 + sub, :], after], axis=0)
        edge_t = lax.broadcasted_iota(jnp.int32, (HALO, 1), 0) + (i * tm + r0)
        ys = []
        for gi, w in enumerate(POOL_WINDOWS):
            lo = w // 2
            hi = w - 1 - lo
            part = ext[:, gi * POOL_GROUP_DIM:(gi + 1) * POOL_GROUP_DIM]
            centre = part[HALO:HALO + sub]
            span = 1
            while 2 * span < w:
                n = part.shape[0] - span
                part = part[:n] + part[span:span + n]
                span *= 2
            tot = part[HALO - lo:HALO - lo + sub] + part[HALO - lo + span:HALO - lo + span + sub]
            d = tot * (1.0 / w) - centre
            fixes = []
            for e0 in (0, sub - HALO):
                tt = edge_t + e0
                cnt = (jnp.minimum(tt + hi, seq - 1) - jnp.maximum(tt - lo, 0) + 1).astype(F32)
                fixes.append(tot[e0:e0 + HALO] / cnt - centre[e0:e0 + HALO])
            d = jnp.concatenate([fixes[0], d[HALO:sub - HALO], fixes[1]], axis=0)
            ys.append(jnp.dot(d.astype(BF16), pw_ref[0, gi], preferred_element_type=F32))
        return (jnp.concatenate(ys, axis=1) * ps_ref[0]).astype(BF16)

    def mixed(r0, y):
        rows = slice(r0, r0 + sub)
        mix = (jnp.dot(y, wo_ref[0, :pwidth], preferred_element_type=F32)
               + jnp.dot(a_ref[0, rows, :], wo_ref[0, pwidth:], preferred_element_type=F32))
        inv = lax.rsqrt(jnp.mean(mix * mix, axis=-1, keepdims=True) + EPS)
        mid_ref[rows, :] = x_ref[0, rows, :] + (mix * inv) * gain
        return (_rms(mid_ref[rows, :], gpre_ref[0]) * (1.0 + mod(4)) + mod(3)).astype(BF16)

    def mlp_chunk(r0, h, c):
        rows = slice(r0, r0 + sub)
        a = jnp.maximum(jnp.dot(h, w1_ref[0, :, c * tf:(c + 1) * tf], preferred_element_type=F32), 0.0)
        part = jnp.dot((a * a).astype(BF16), w2_ref[0, c * tf:(c + 1) * tf, :], preferred_element_type=F32)
        if c == 0:
            acc_ref[rows, :] = part
        else:
            acc_ref[rows, :] += part

    n_chunks = w1_ref.shape[2] // tf
    h = mixed(0, pooled(0))
    for r0 in range(0, tm, sub):
        rows = slice(r0, r0 + sub)
        more = r0 + sub < tm
        for c in range(n_chunks):
            mlp_chunk(r0, h, c)
            if more and c == 0:
                y_next = pooled(r0 + sub)
            if more and c == max(n_chunks - 2, 0):
                h_next = mixed(r0 + sub, y_next)
        o_ref[0, rows, :] = mid_ref[rows, :] + mod(5) * _rms(acc_ref[rows, :], gpost_ref[0])
        if more:
            h = h_next


def _block_tail(u, attn, x, mods, li, row_of, g_mix, g_pre, g_post, pool_w, pool_scale, w_out, w1, w2, tm, tf=1024):
    b, l, d = x.shape
    pwidth, aw = u.shape[-1], attn.shape[-1]
    assert max(POOL_WINDOWS) // 2 <= HALO and tm % HALO == 0
    nh = tm // HALO
    last = l // HALO - 1
    tok = lambda w: pl.BlockSpec((1, tm, w), lambda i, j: (i, j, 0))
    return pl.pallas_call(
        functools.partial(_tail_kernel, seq=l, tf=tf),
        grid=(b, l // tm),
        in_specs=[tok(pwidth),
                  pl.BlockSpec((1, HALO, pwidth), lambda i, j: (i, jnp.maximum(j * nh - 1, 0), 0)),
                  pl.BlockSpec((1, HALO, pwidth), lambda i, j: (i, jnp.minimum((j + 1) * nh, last), 0)),
                  tok(aw), tok(d), _mods_spec(mods, li, row_of),
                  _layer(g_mix, li), _layer(g_pre, li), _layer(g_post, li),
                  _layer(pool_w, li), _layer(pool_scale, li), _layer(w_out, li), _layer(w1, li), _layer(w2, li)],
        out_specs=tok(d),
        out_shape=jax.ShapeDtypeStruct((b, l, d), F32),
        scratch_shapes=[pltpu.VMEM((tm, d), F32), pltpu.VMEM((tm, d), F32)],
        compiler_params=_cparams(2),
        name="block_tail",
    )(u, u, u, attn, x, mods, g_mix, g_pre, g_post, pool_w, pool_scale, w_out, w1, w2)


def kernel(x, c, ctx, c_ctx, w_mod, b_mod, g_pre_mix, g_post_mix, g_pre_mlp, g_post_mlp,
           w_in, pool_w, pool_scale, rpb, w_out, w_mlp_in, w_mlp_out):
    b, l, d = x.shape
    depth = w_mod.shape[0]
    n_ctx = ctx.shape[1]
    pool_width = pool_scale.shape[-1]
    aw = (w_in.shape[-1] - pool_width) // 3
    rows = l // GRID_W
    tm_in = min(1024, l)
    tm_tail = min(1024, l)

    mod_rows = -(-(b + 1) // 8) * 8
    cc = jnp.zeros((mod_rows, d), F32).at[:b].set(c).at[b].set(c_ctx)
    mods = _modulation(cc, w_mod, b_mod).reshape(depth, mod_rows, N_MOD, d)
    x_row, ctx_row = (lambda i: i), (lambda i: b)

    w_in_b = w_in.astype(BF16)
    w_uk = jnp.concatenate([w_in_b[..., :pool_width], w_in_b[..., pool_width + aw:pool_width + 2 * aw]], axis=-1)
    w_qv = jnp.swapaxes(jnp.concatenate([w_in_b[..., pool_width:pool_width + aw],
                                         w_in_b[..., pool_width + 2 * aw:]], axis=-1), 1, 2)
    pool_w_b, w_out_b = pool_w.astype(BF16), w_out.astype(BF16)
    w1_b, w2_b = w_mlp_in.astype(BF16), w_mlp_out.astype(BF16)
    vec = lambda a: a.reshape(depth, 1, a.shape[-1])
    g_pre_mix, g_post_mix, g_pre_mlp, g_post_mlp, pool_scale = map(
        vec, (g_pre_mix, g_post_mix, g_pre_mlp, g_post_mlp, pool_scale))

    info, types, kh, kr = _na_structure(rows)
    info = jnp.asarray(info)

    for li in range(depth):
        tail = functools.partial(_block_tail, mods=mods, li=li, g_mix=g_post_mix, g_pre=g_pre_mlp, g_post=g_post_mlp,
                                 pool_w=pool_w_b, pool_scale=pool_scale, w_out=w_out_b, w1=w1_b, w2=w2_b)
        u, qt, k, vt = _in_projection(x, mods, li, x_row, g_pre_mix, w_uk, w_qv, pool_width, tm_in)
        uc, qct, kc, vct = _in_projection(ctx, mods, li, ctx_row, g_pre_mix, w_uk, w_qv, pool_width, n_ctx)
        table = _na_bias_table(rpb, li, types, kh, kr)
        attn = _neighbourhood_attention(qt, k, vt, kc, vct, table, info)
        x = tail(u, attn, x, row_of=x_row, tm=tm_tail)
        if li != depth - 1:
            ctx = tail(uc, _context_attention(qct, kc, vct), ctx, row_of=ctx_row, tm=n_ctx)
    return x
```

```python
import functools

import numpy as np
import jax
import jax.numpy as jnp
from jax import lax
from jax.experimental import pallas as pl
from jax.experimental.pallas import tpu as pltpu

F32 = jnp.float32
BF16 = jnp.bfloat16

EPS = 1e-6
GRID_W = 64
N_HEADS = 8
HEAD_DIM = 64
NA_KH = 8
NA_KW = 16
POOL_WINDOWS = (2, 4, 8, 16)
POOL_GROUP_DIM = 128
N_MOD = 6

LANES = 128
BF16_ROWS = 16
HALO = 8
NA_RB = 2
NA_ROWS_PER_STEP = 16
NA_SCORES_AHEAD = 3
MIX_SUB = 256
PV_ROWS = LANES + BF16_ROWS
VMEM_LIMIT = 56 * 1024 * 1024

LOG2E = 1.4426950408889634
Q_SCALE = HEAD_DIM ** -0.5 * LOG2E

_NT = (((1,), (1,)), ((), ()))
N_PAIRS = N_HEADS // 2


def _cparams(n_axes):
    return pltpu.CompilerParams(dimension_semantics=("parallel",) * n_axes,
                                vmem_limit_bytes=VMEM_LIMIT)


def _layer(arr, li):
    nd = arr.ndim - 1
    return pl.BlockSpec((1,) + arr.shape[1:], lambda *_: (li,) + (0,) * nd, pipeline_mode=pl.Buffered(1))


def _mods_spec(mods, li, row_of):
    return pl.BlockSpec((1, 1) + mods.shape[2:], lambda i, j: (li, row_of(i), 0, 0))


def _rms(x, g):
    return x * lax.rsqrt(jnp.mean(x * x, axis=-1, keepdims=True) + EPS) * g


def _mod_kernel(cc_ref, w_ref, b_ref, o_ref):
    a = cc_ref[...]
    a = a / (1.0 + jnp.exp(-a))
    o_ref[0] = jnp.dot(a.astype(BF16), w_ref[0].astype(BF16), preferred_element_type=F32) + b_ref[0]


def _modulation(cc, w_mod, b_mod):
    depth, d, n = w_mod.shape
    rows = cc.shape[0]
    tn = 1536
    return pl.pallas_call(
        _mod_kernel,
        grid=(depth, n // tn),
        in_specs=[pl.BlockSpec((rows, d), lambda l, j: (0, 0)),
                  pl.BlockSpec((1, d, tn), lambda l, j: (l, 0, j)),
                  pl.BlockSpec((1, 1, tn), lambda l, j: (l, 0, j))],
        out_specs=pl.BlockSpec((1, rows, tn), lambda l, j: (l, 0, j)),
        out_shape=jax.ShapeDtypeStruct((depth, rows, n), F32),
        compiler_params=_cparams(2),
        name="adaln_mod",
    )(cc, w_mod, b_mod.reshape(depth, 1, n))


def _inproj_kernel(x_ref, m_ref, g_ref, wuk_ref, wqv_ref, u_ref, qt_ref, k_ref, vt_ref):
    pw = u_ref.shape[-1]
    aw = k_ref.shape[-1]
    tm = x_ref.shape[1]
    h = _rms(x_ref[0], g_ref[0]) * (1.0 + m_ref[0, 0, 1:2, :]) + m_ref[0, 0, 0:1, :]
    h = h.astype(BF16)
    u_ref[0] = jnp.dot(h, wuk_ref[0, :, :pw], preferred_element_type=F32)
    k_ref[0] = jnp.dot(h, wuk_ref[0, :, pw:], preferred_element_type=F32).astype(BF16)
    qv = lax.dot_general(wqv_ref[0], h, _NT, preferred_element_type=F32)
    qt_ref[0] = (qv[:aw] * Q_SCALE).astype(BF16)
    ones = jnp.ones((BF16_ROWS, tm), BF16)
    for hp in range(N_PAIRS):
        vt_ref[0, hp * PV_ROWS:hp * PV_ROWS + LANES, :] = qv[aw + hp * LANES:aw + (hp + 1) * LANES].astype(BF16)
        vt_ref[0, hp * PV_ROWS + LANES:(hp + 1) * PV_ROWS, :] = ones


def _in_projection(x, mods, li, row_of, g, w_uk, w_qv, pool_width, tm):
    b, l, d = x.shape
    aw = w_uk.shape[-1] - pool_width
    assert aw == N_PAIRS * LANES
    tok = lambda w: pl.BlockSpec((1, tm, w), lambda i, j: (i, j, 0))
    feat = lambda r: pl.BlockSpec((1, r, tm), lambda i, j: (i, 0, j))
    return pl.pallas_call(
        _inproj_kernel,
        grid=(b, l // tm),
        in_specs=[tok(d), _mods_spec(mods, li, row_of), _layer(g, li), _layer(w_uk, li), _layer(w_qv, li)],
        out_specs=[tok(pool_width), feat(aw), tok(aw), feat(N_PAIRS * PV_ROWS)],
        out_shape=[jax.ShapeDtypeStruct((b, l, pool_width), F32),
                   jax.ShapeDtypeStruct((b, aw, l), BF16),
                   jax.ShapeDtypeStruct((b, l, aw), BF16),
                   jax.ShapeDtypeStruct((b, N_PAIRS * PV_ROWS, l), BF16)],
        compiler_params=_cparams(2),
        name="in_proj",
    )(x, mods, g, w_uk, w_qv)


def _na_structure(rows):
    kh = min(NA_KH, rows)
    step = LANES // GRID_W
    kr = -(-(NA_RB + kh - 1 + step - 1) // step) * step
    assert rows % NA_RB == 0 and rows % step == 0 and rows >= kr and GRID_W >= NA_KW
    types, info = {}, np.zeros((2, rows // NA_RB), np.int32)
    for p in range(rows // NA_RB):
        r = p * NA_RB + np.arange(NA_RB)
        rs = np.clip(r - kh // 2, 0, rows - kh)
        rs0 = int(np.clip(rs.min() // step * step, 0, rows - kr))
        assert rs0 % step == 0 and np.all(rs >= rs0) and np.all(rs + kh <= rs0 + kr)
        key = (tuple(int(a) for a in rs - rs0), tuple(int(a) for a in rs - r))
        info[0, p] = types.setdefault(key, len(types))
        info[1, p] = rs0
    return info, types, kh, kr


def _bias_table_kernel(rpb_ref, o_ref, *, types, kh, kr):
    n_dr, n_dc = 2 * NA_KH - 1, 2 * NA_KW - 1
    kc = lax.broadcasted_iota(jnp.int32, (GRID_W, GRID_W), 0)
    cq = lax.broadcasted_iota(jnp.int32, (GRID_W, GRID_W), 1)
    q_start = jnp.clip(cq - NA_KW // 2, 0, GRID_W - NA_KW)
    col_ok = (kc >= q_start) & (kc < q_start + NA_KW)
    dc = kc - cq + (NA_KW - 1)
    neg = jnp.full((GRID_W, GRID_W), -jnp.inf, F32)
    for s in range(2):
        h = pl.program_id(0) * 2 + s
        tiles = []
        for dr in range(n_dr):
            e = neg
            for j in range(n_dc):
                e = jnp.where(dc == j, rpb_ref[(h * n_dr + dr) * n_dc + j] * LOG2E, e)
            tiles.append(jnp.where(col_ok, e, neg))
        for (off, rel), t in types.items():
            for tq in range(NA_RB):
                blocks = [tiles[rel[tq] + (j - off[tq]) + NA_KH - 1] if off[tq] <= j < off[tq] + kh else neg
                          for j in range(kr)]
                col0 = (s * NA_RB + tq) * GRID_W
                o_ref[0, t, :, col0:col0 + GRID_W] = jnp.concatenate(blocks, axis=0)


def _na_bias_table(rpb, li, types, kh, kr):
    depth, h = rpb.shape[:2]
    shape = (h // 2, len(types), kr * GRID_W, 2 * NA_RB * GRID_W)
    return pl.pallas_call(
        functools.partial(_bias_table_kernel, types=types, kh=kh, kr=kr),
        grid=(h // 2,),
        in_specs=[pl.BlockSpec(memory_space=pltpu.SMEM)],
        out_specs=pl.BlockSpec((1,) + shape[1:], lambda i: (i, 0, 0, 0)),
        out_shape=jax.ShapeDtypeStruct(shape, F32),
        compiler_params=_cparams(1),
        name="na_bias_table",
    )(rpb[li].reshape(-1))


def _head_row_masks(n):
    first = lax.broadcasted_iota(jnp.int32, (LANES, n), 0) < HEAD_DIM
    return first.astype(F32).astype(BF16), (~first).astype(F32).astype(BF16)


def _pair_scores(qt2, masks, keys):
    qq = jnp.concatenate([qt2 * masks[0], qt2 * masks[1]], axis=1)
    out = []
    for k2, bias in keys:
        s = jnp.dot(k2, qq, preferred_element_type=F32)
        out.append(s if bias is None else s + bias)
    return out


def _pair_output(scores, values):
    n = scores[0].shape[1] // 2
    m = functools.reduce(jnp.maximum, [jnp.max(s, axis=0, keepdims=True) for s in scores])
    o = functools.reduce(jnp.add, [jnp.dot(vt, jnp.exp2(s - m).astype(BF16), preferred_element_type=F32)
                                   for s, vt in zip(scores, values)])
    o = o[:LANES] * (1.0 / o[LANES:LANES + 1])
    return jnp.concatenate([o[:HEAD_DIM, :n], o[HEAD_DIM:, n:]], axis=0).T


def _natten_kernel(info_ref, qt_ref, k_ref, vt_ref, kc_ref, vct_ref, tab_ref, o_ref):
    nq = NA_RB * GRID_W
    nk = tab_ref.shape[-2]
    iters = qt_ref.shape[2] // nq
    masks = _head_row_masks(nq)
    g = pl.program_id(1)

    koffs = [pl.multiple_of(info_ref[1, g * iters + j] * GRID_W, LANES) for j in range(iters)]
    units = [(j, hp) for j in range(iters) for hp in range(N_PAIRS)]

    def scores(j, hp):
        fs = slice(hp * LANES, (hp + 1) * LANES)
        bias = tab_ref[hp, info_ref[0, g * iters + j]]
        return _pair_scores(qt_ref[0, fs, j * nq:(j + 1) * nq], masks,
                            [(k_ref[0, pl.ds(koffs[j], nk), fs], bias), (kc_ref[0, :, fs], None)])

    queue = [scores(*units[n]) for n in range(NA_SCORES_AHEAD)]
    for n, (j, hp) in enumerate(units):
        cur = queue.pop(0)
        if n + NA_SCORES_AHEAD < len(units):
            queue.append(scores(*units[n + NA_SCORES_AHEAD]))
        vs = slice(hp * PV_ROWS, (hp + 1) * PV_ROWS)
        out = _pair_output(cur, [vt_ref[0, vs, pl.ds(koffs[j], nk)], vct_ref[0, vs, :]])
        o_ref[0, j * nq:(j + 1) * nq, hp * LANES:(hp + 1) * LANES] = out.astype(BF16)


def _neighbourhood_attention(qt, k, vt, kc, vct, table, info):
    b, l, aw = k.shape
    c = kc.shape[1]
    assert NA_RB * GRID_W == LANES
    tq = NA_ROWS_PER_STEP * GRID_W
    per_batch = lambda arr: pl.BlockSpec((1,) + arr.shape[1:], lambda i, j, s: (i, 0, 0))
    grid_spec = pltpu.PrefetchScalarGridSpec(
        num_scalar_prefetch=1,
        grid=(b, l // tq),
        in_specs=[pl.BlockSpec((1, aw, tq), lambda i, j, s: (i, 0, j)),
                  per_batch(k), per_batch(vt), per_batch(kc), per_batch(vct),
                  pl.BlockSpec(table.shape, lambda i, j, s: (0, 0, 0, 0), pipeline_mode=pl.Buffered(1))],
        out_specs=pl.BlockSpec((1, tq, aw), lambda i, j, s: (i, j, 0)))
    return pl.pallas_call(
        _natten_kernel,
        grid_spec=grid_spec,
        out_shape=jax.ShapeDtypeStruct((b, l, aw), BF16),
        compiler_params=_cparams(2),
        name="natten",
    )(info, qt, k, vt, kc, vct, table)


def _ctx_attn_kernel(qt_ref, k_ref, vt_ref, o_ref):
    masks = _head_row_masks(qt_ref.shape[2])
    for hp in range(N_PAIRS):
        fs = slice(hp * LANES, (hp + 1) * LANES)
        s = _pair_scores(qt_ref[0, fs, :], masks, [(k_ref[0, :, fs], None)])
        o_ref[0, :, fs] = _pair_output(s, [vt_ref[0, hp * PV_ROWS:(hp + 1) * PV_ROWS, :]]).astype(BF16)


def _context_attention(qt, k, vt):
    b, c, aw = k.shape
    whole = lambda arr: pl.BlockSpec((1,) + arr.shape[1:], lambda i: (i, 0, 0))
    return pl.pallas_call(
        _ctx_attn_kernel,
        grid=(b,),
        in_specs=[whole(qt), whole(k), whole(vt)],
        out_specs=whole(k),
        out_shape=jax.ShapeDtypeStruct((b, c, aw), BF16),
        compiler_params=_cparams(1),
        name="ctx_attn",
    )(qt, k, vt)


def _tail_kernel(u_ref, up_ref, un_ref, a_ref, x_ref, m_ref, gmix_ref, gpre_ref, gpost_ref, pw_ref, ps_ref,
                 wo_ref, w1_ref, w2_ref, o_ref, mid_ref, acc_ref, *, seq, tf):
    tm, pwidth = u_ref.shape[1], u_ref.shape[2]
    sub = min(tm, MIX_SUB)
    i = pl.program_id(1)
    prev = jnp.where(i > 0, up_ref[0], 0.0)
    nxt = jnp.where(i < pl.num_programs(1) - 1, un_ref[0], 0.0)
    mod = lambda k: m_ref[0, 0, k:k + 1, :]
    gain = gmix_ref[0] * mod(2)

    def pooled(r0):
        before = prev if r0 == 0 else u_ref[0, r0 - HALO:r0, :]
        after = nxt if r0 + sub == tm else u_ref[0, r0 + sub:r0 + sub + HALO, :]
        ext = jnp.concatenate([before, u_ref[0, r0:r0 + sub, :], after], axis=0)
        edge_t = lax.broadcasted_iota(jnp.int32, (HALO, 1), 0) + (i * tm + r0)
        ys = []
        for gi, w in enumerate(POOL_WINDOWS):
            lo = w // 2
            hi = w - 1 - lo
            part = ext[:, gi * POOL_GROUP_DIM:(gi + 1) * POOL_GROUP_DIM]
            centre = part[HALO:HALO + sub]
            span = 1
            while 2 * span < w:
                n = part.shape[0] - span
                part = part[:n] + part[span:span + n]
                span *= 2
            tot = part[HALO - lo:HALO - lo + sub] + part[HALO - lo + span:HALO - lo + span + sub]
            d = tot * (1.0 / w) - centre
            fixes = []
            for e0 in (0, sub - HALO):
                tt = edge_t + e0
                cnt = (jnp.minimum(tt + hi, seq - 1) - jnp.maximum(tt - lo, 0) + 1).astype(F32)
                fixes.append(tot[e0:e0 + HALO] / cnt - centre[e0:e0 + HALO])
            d = jnp.concatenate([fixes[0], d[HALO:sub - HALO], fixes[1]], axis=0)
            ys.append(jnp.dot(d.astype(BF16), pw_ref[0, gi], preferred_element_type=F32))
        return (jnp.concatenate(ys, axis=1) * ps_ref[0]).astype(BF16)

    def mixed(r0, y):
        rows = slice(r0, r0 + sub)
        mix = (jnp.dot(y, wo_ref[0, :pwidth], preferred_element_type=F32)
               + jnp.dot(a_ref[0, rows, :], wo_ref[0, pwidth:], preferred_element_type=F32))
        inv = lax.rsqrt(jnp.mean(mix * mix, axis=-1, keepdims=True) + EPS)
        mid_ref[rows, :] = x_ref[0, rows, :] + (mix * inv) * gain
        return (_rms(mid_ref[rows, :], gpre_ref[0]) * (1.0 + mod(4)) + mod(3)).astype(BF16)

    def mlp_chunk(r0, h, c):
        rows = slice(r0, r0 + sub)
        a = jnp.maximum(jnp.dot(h, w1_ref[0, :, c * tf:(c + 1) * tf], preferred_element_type=F32), 0.0)
        part = jnp.dot((a * a).astype(BF16), w2_ref[0, c * tf:(c + 1) * tf, :], preferred_element_type=F32)
        if c == 0:
            acc_ref[rows, :] = part
        else:
            acc_ref[rows, :] += part

    n_chunks = w1_ref.shape[2] // tf
    h = mixed(0, pooled(0))
    for r0 in range(0, tm, sub):
        rows = slice(r0, r0 + sub)
        more = r0 + sub < tm
        for c in range(n_chunks):
            mlp_chunk(r0, h, c)
            if more and c == 0:
                y_next = pooled(r0 + sub)
            if more and c == max(n_chunks - 2, 0):
                h_next = mixed(r0 + sub, y_next)
        o_ref[0, rows, :] = mid_ref[rows, :] + mod(5) * _rms(acc_ref[rows, :], gpost_ref[0])
        if more:
            h = h_next


def _block_tail(u, attn, x, mods, li, row_of, g_mix, g_pre, g_post, pool_w, pool_scale, w_out, w1, w2, tm, tf=1024):
    b, l, d = x.shape
    pwidth, aw = u.shape[-1], attn.shape[-1]
    assert max(POOL_WINDOWS) // 2 <= HALO and tm % HALO == 0
    nh = tm // HALO
    last = l // HALO - 1
    tok = lambda w: pl.BlockSpec((1, tm, w), lambda i, j: (i, j, 0))
    return pl.pallas_call(
        functools.partial(_tail_kernel, seq=l, tf=tf),
        grid=(b, l // tm),
        in_specs=[tok(pwidth),
                  pl.BlockSpec((1, HALO, pwidth), lambda i, j: (i, jnp.maximum(j * nh - 1, 0), 0)),
                  pl.BlockSpec((1, HALO, pwidth), lambda i, j: (i, jnp.minimum((j + 1) * nh, last), 0)),
                  tok(aw), tok(d), _mods_spec(mods, li, row_of),
                  _layer(g_mix, li), _layer(g_pre, li), _layer(g_post, li),
                  _layer(pool_w, li), _layer(pool_scale, li), _layer(w_out, li), _layer(w1, li), _layer(w2, li)],
        out_specs=tok(d),
        out_shape=jax.ShapeDtypeStruct((b, l, d), F32),
        scratch_shapes=[pltpu.VMEM((tm, d), F32), pltpu.VMEM((tm, d), F32)],
        compiler_params=_cparams(2),
        name="block_tail",
    )(u, u, u, attn, x, mods, g_mix, g_pre, g_post, pool_w, pool_scale, w_out, w1, w2)


def kernel(x, c, ctx, c_ctx, w_mod, b_mod, g_pre_mix, g_post_mix, g_pre_mlp, g_post_mlp,
           w_in, pool_w, pool_scale, rpb, w_out, w_mlp_in, w_mlp_out):
    b, l, d = x.shape
    depth = w_mod.shape[0]
    n_ctx = ctx.shape[1]
    pool_width = pool_scale.shape[-1]
    aw = (w_in.shape[-1] - pool_width) // 3
    rows = l // GRID_W
    tm_in = min(1024, l)
    tm_tail = min(1024, l)

    mod_rows = -(-(b + 1) // 8) * 8
    cc = jnp.zeros((mod_rows, d), F32).at[:b].set(c).at[b].set(c_ctx)
    mods = _modulation(cc, w_mod, b_mod).reshape(depth, mod_rows, N_MOD, d)
    x_row, ctx_row = (lambda i: i), (lambda i: b)

    w_in_b = w_in.astype(BF16)
    w_uk = jnp.concatenate([w_in_b[..., :pool_width], w_in_b[..., pool_width + aw:pool_width + 2 * aw]], axis=-1)
    w_qv = jnp.swapaxes(jnp.concatenate([w_in_b[..., pool_width:pool_width + aw],
                                         w_in_b[..., pool_width + 2 * aw:]], axis=-1), 1, 2)
    pool_w_b, w_out_b = pool_w.astype(BF16), w_out.astype(BF16)
    w1_b, w2_b = w_mlp_in.astype(BF16), w_mlp_out.astype(BF16)
    vec = lambda a: a.reshape(depth, 1, a.shape[-1])
    g_pre_mix, g_post_mix, g_pre_mlp, g_post_mlp, pool_scale = map(
        vec, (g_pre_mix, g_post_mix, g_pre_mlp, g_post_mlp, pool_scale))

    info, types, kh, kr = _na_structure(rows)
    info = jnp.asarray(info)

    for li in range(depth):
        tail = functools.partial(_block_tail, mods=mods, li=li, g_mix=g_post_mix, g_pre=g_pre_mlp, g_post=g_post_mlp,
                                 pool_w=pool_w_b, pool_scale=pool_scale, w_out=w_out_b, w1=w1_b, w2=w2_b)
        u, qt, k, vt = _in_projection(x, mods, li, x_row, g_pre_mix, w_uk, w_qv, pool_width, tm_in)
        uc, qct, kc, vct = _in_projection(ctx, mods, li, ctx_row, g_pre_mix, w_uk, w_qv, pool_width, n_ctx)
        table = _na_bias_table(rpb, li, types, kh, kr)
        attn = _neighbourhood_attention(qt, k, vt, kc, vct, table, info)
        x = tail(u, attn, x, row_of=x_row, tm=tm_tail)
        if li != depth - 1:
            ctx = tail(uc, _context_attention(qct, kc, vct), ctx, row_of=ctx_row, tm=n_ctx)
    return x
```
